```python
import jax, jax.numpy as jnp
from jax import lax
import numpy as np

D_MODEL = 1024
BATCH = 4
SEQ = 4096
DEPTH = 4

GRID_W = 64
CTX_LEN = 256
N_MIXERS = 2
N_A_LAYERS = (DEPTH + N_MIXERS - 1) // N_MIXERS
N_B_LAYERS = DEPTH // N_MIXERS

MLA_HEADS = 8
MLA_NOPE = 128
MLA_ROPE = 64
MLA_V = 128
MLA_Q_LORA = 256
MLA_KV_LORA = 128
MLA_SCALE = (MLA_NOPE + MLA_ROPE) ** -0.5
ROPE_THETA = 10000.0
Q_BLOCK = 128

RW_HEAD = 64
RW_HEADS = D_MODEL // RW_HEAD
RW_DECAY_LORA = 64
RW_AAA_LORA = 64
RW_MV_LORA = 32
RW_GATE_LORA = 128
RW_LNX_EPS = 64e-5

N_EXPERTS = 16
CAPACITY_FACTOR = 2
EXPERT_FF = 1408

RMS_EPS = 1e-6
F32 = jnp.float32

kernel_name = 'hybrid_mla_rwkv7_ecmoe_diffusion_block'


def rmsnorm(x, g):
    x32 = x.astype(F32)
    y = x32 * lax.rsqrt(jnp.mean(x32 * x32, axis=-1, keepdims=True) + RMS_EPS)
    return (y * g.astype(F32)).astype(x.dtype)


def axial_rope_tables(rows):
    r = jnp.broadcast_to(jnp.arange(rows)[:, None], (rows, GRID_W)).reshape(-1).astype(F32)
    col = jnp.broadcast_to(jnp.arange(GRID_W)[None, :], (rows, GRID_W)).reshape(-1).astype(F32)
    n_freq = MLA_ROPE // 4
    inv = ROPE_THETA ** (-jnp.arange(n_freq, dtype=F32) / n_freq)
    ang = jnp.concatenate([r[:, None] * inv, col[:, None] * inv], axis=-1)
    return jnp.cos(ang), jnp.sin(ang)


def apply_rope(x, cos, sin):
    half = x.shape[-1] // 2
    x1, x2 = x[..., :half], x[..., half:]
    return jnp.concatenate([x1 * cos - x2 * sin, x1 * sin + x2 * cos], axis=-1).astype(x.dtype)


def mla_project(h, w_in, g_q, g_kv, w_uq, w_ukv, rope):
    B, T, _ = h.shape
    z = h @ w_in
    cq, ckv, k_rope = jnp.split(z, [MLA_Q_LORA, MLA_Q_LORA + MLA_KV_LORA], axis=-1)
    q = (rmsnorm(cq, g_q) @ w_uq).reshape(B, T, MLA_HEADS, MLA_NOPE + MLA_ROPE)
    kv = (rmsnorm(ckv, g_kv) @ w_ukv).reshape(B, T, MLA_HEADS, MLA_NOPE + MLA_V)
    q_nope, q_rope = q[..., :MLA_NOPE], q[..., MLA_NOPE:]
    k_nope, v = kv[..., :MLA_NOPE], kv[..., MLA_NOPE:]
    if rope is not None:
        cos, sin = rope
        q_rope = apply_rope(q_rope, cos[:, None, :], sin[:, None, :])
        k_rope = apply_rope(k_rope, cos, sin)
    return q_nope, q_rope, k_nope, k_rope, v


def mla_attend(q_nope, q_rope, k_nope, k_rope, v):
    s = (jnp.einsum('bqhn,bkhn->bhqk', q_nope, k_nope)
         + jnp.einsum('bqhr,bkr->bhqk', q_rope, k_rope))
    p = jax.nn.softmax(s.astype(F32) * MLA_SCALE, axis=-1).astype(v.dtype)
    return jnp.einsum('bhqk,bkhv->bqhv', p, v)


def blocked_attend(q_nope, q_rope, k_nope, k_rope, v):
    B, T, H, _ = q_nope.shape
    nb = T // Q_BLOCK

    def to_blocks(a):
        return jnp.moveaxis(a.reshape(B, nb, Q_BLOCK, *a.shape[2:]), 1, 0)

    out = lax.map(lambda qs: mla_attend(qs[0], qs[1], k_nope, k_rope, v),
                  (to_blocks(q_nope), to_blocks(q_rope)))
    return jnp.moveaxis(out, 0, 1).reshape(B, T, H * MLA_V)


def mla_mixer(h_lat, h_ctx, w_in, g_q, g_kv, w_uq, w_ukv, w_o, rope, need_ctx):
    qn_c, qr_c, kn_c, kr_c, v_c = mla_project(h_ctx, w_in, g_q, g_kv, w_uq, w_ukv, None)
    qn_l, qr_l, kn_l, kr_l, v_l = mla_project(h_lat, w_in, g_q, g_kv, w_uq, w_ukv, rope)
    kn = jnp.concatenate([kn_c, kn_l], axis=1)
    kr = jnp.concatenate([kr_c, kr_l], axis=1)
    vv = jnp.concatenate([v_c, v_l], axis=1)
    y_lat = blocked_attend(qn_l, qr_l, kn, kr, vv) @ w_o
    y_ctx = None
    if need_ctx:
        B, L = h_ctx.shape[:2]
        y_ctx = mla_attend(qn_c, qr_c, kn_c, kr_c, v_c).reshape(B, L, MLA_HEADS * MLA_V) @ w_o
    return y_lat, y_ctx


def token_shift_centred(x):
    prev = jnp.pad(x[:, :-1], ((0, 0), (1, 0), (0, 0)))
    nxt = jnp.pad(x[:, 1:], ((0, 0), (0, 1), (0, 0)))
    return 0.5 * (prev + nxt) - x


def to_heads(t):
    return t.reshape(*t.shape[:-1], RW_HEADS, RW_HEAD)


def rwkv_stream(h, mu, w_rkv, w0, w1, w2, a0, a1, a2, g1, g2, k_k, k_a, vres):
    B, T, D = h.shape
    xx = token_shift_centred(h)
    xs = h[None] + xx[None] * mu[:, None, None, :]
    r, k, v = jnp.einsum('mbtd,mde->mbte', xs[:3], w_rkv)
    xw, xa, xg = xs[3], xs[4], xs[5]
    if vres is not None:
        v_first, v0, v1, v2 = vres
        v = v + (v_first - v) * jax.nn.sigmoid(v0 + (xs[2] @ v1) @ v2)
    wz = w0[:, None, None, :] + jnp.einsum('zbtl,zld->zbtd', jnp.tanh(jnp.einsum('btd,zdl->zbtl', xw, w1)), w2)
    decay = jnp.exp(-jnp.exp(-jax.nn.softplus(-wz.astype(F32)) - 0.5))
    az = a0[:, None, None, :] + jnp.einsum('zbtl,zld->zbtd', jnp.einsum('btd,zdl->zbtl', xa, a1), a2)
    a = jax.nn.sigmoid(az.astype(F32))
    g = jax.nn.sigmoid(xg @ g1) @ g2
    kk = to_heads((k * k_k).astype(F32))
    kk = kk / jnp.maximum(jnp.sqrt(jnp.sum(kk * kk, axis=-1, keepdims=True)), 1e-12)
    k_dir = k.astype(F32)[None] * (1.0 + (a - 1.0) * k_a.astype(F32))
    return r, v, g, to_heads(decay), to_heads(a), to_heads(k_dir), kk


def wkv_scan(state0, r, w, k, v, kk, a, reverse):
    seq = tuple(jnp.moveaxis(t.astype(F32), 1, 0) for t in (r, w, k, v, kk, kk * a))

    def step(S, inp):
        r_t, w_t, k_t, v_t, kk_t, b_t = inp
        sa = jnp.einsum('bhvk,bhk->bhv', S, -kk_t)
        S = S * w_t[:, :, None, :] + sa[..., None] * b_t[:, :, None, :] + v_t[..., None] * k_t[:, :, None, :]
        return S, jnp.einsum('bhvk,bhk->bhv', S, r_t)

    s_final, ys = lax.scan(step, state0, seq, reverse=reverse)
    return s_final, jnp.moveaxis(ys, 0, 1)


def rwkv_output(y, r, k_dir, v, g, r_k, lnx_g, lnx_b, w_o):
    B, T, H, N = y.shape
    mean = jnp.mean(y, axis=-1, keepdims=True)
    var = jnp.mean(jnp.square(y - mean), axis=-1, keepdims=True)
    yn = ((y - mean) * lax.rsqrt(var + RW_LNX_EPS)).reshape(B, T, H * N) * lnx_g + lnx_b
    bonus = jnp.sum(to_heads(r).astype(F32) * (k_dir[0] + k_dir[1]) * r_k, axis=-1, keepdims=True) * to_heads(v).astype(F32)
    out = ((yn + bonus.reshape(B, T, H * N)) * g).astype(g.dtype)
    return out @ w_o


def rwkv_mixer(h_lat, h_ctx, mu, w_rkv, w0, w1, w2, a0, a1, a2, g1, g2, k_k, k_a, r_k,
               lnx_g, lnx_b, w_o, vres, v_first, need_ctx):
    shared = (mu, w_rkv, w0, w1, w2, a0, a1, a2, g1, g2, k_k, k_a)
    vres_c = None if vres is None else (v_first[0],) + vres
    vres_l = None if vres is None else (v_first[1],) + vres
    rc, vc, gc, dc, ac, kc, kkc = rwkv_stream(h_ctx, *shared, vres_c)
    rl, vl, gl, dl, al, kl, kkl = rwkv_stream(h_lat, *shared, vres_l)
    s0 = jnp.zeros((h_lat.shape[0], RW_HEADS, RW_HEAD, RW_HEAD), F32)
    y_ctx = 0.0
    y_lat = 0.0
    for z in range(2):
        rev = z == 1
        s_ctx, yc = wkv_scan(s0, to_heads(rc), dc[z], kc[z], to_heads(vc), kkc, ac[z], rev)
        _, yl = wkv_scan(s_ctx, to_heads(rl), dl[z], kl[z], to_heads(vl), kkl, al[z], rev)
        y_ctx = y_ctx + yc
        y_lat = y_lat + yl
    out_lat = rwkv_output(y_lat, rl, kl, vl, gl, r_k, lnx_g, lnx_b, w_o)
    out_ctx = rwkv_output(y_ctx, rc, kc, vc, gc, r_k, lnx_g, lnx_b, w_o) if need_ctx else None
    return out_lat, out_ctx, (vc, vl)


def expert_choice_moe(h, router, w1, w3, w2):
    B, T, D = h.shape
    cap = CAPACITY_FACTOR * T // N_EXPERTS
    aff = jax.nn.softmax((h @ router).astype(F32), axis=-1)
    gate, idx = lax.top_k(jnp.swapaxes(aff, 1, 2), cap)
    bidx = jnp.arange(B)[:, None, None]
    xe = h[bidx, idx]
    hid = jax.nn.silu(jnp.einsum('becd,edf->becf', xe, w1)) * jnp.einsum('becd,edf->becf', xe, w3)
    ye = jnp.einsum('becf,efd->becd', hid, w2) * gate[..., None].astype(h.dtype)
    return jnp.zeros_like(h).at[bidx, idx].add(ye)


def setup_inputs(seed: int = 0) -> dict:
    key = jax.random.key(seed)
    ks = iter(jax.random.split(key, 48))
    D = D_MODEL

    def nrm(shape, scale):
        return jax.random.normal(next(ks), shape, F32) * scale

    def uni(shape, lo, hi):
        return jax.random.uniform(next(ks), shape, F32, lo, hi)

    nv = max(N_B_LAYERS - 1, 0)
    return {
        'x': nrm((BATCH, SEQ, D), 1.0),
        'c': nrm((BATCH, D), 1.0),
        'ctx': nrm((BATCH, CTX_LEN, D), 1.0),
        'c_ctx': nrm((D,), 1.0),
        'mod_w': nrm((DEPTH, D, 6 * D), D ** -0.5),
        'mod_b': nrm((DEPTH, 6 * D), 0.02),
        'norm1_g': 1.0 + nrm((DEPTH, D), 0.02),
        'norm2_g': 1.0 + nrm((DEPTH, D), 0.02),
        'final_g': 1.0 + nrm((D,), 0.02),
        'mla_w_in': nrm((N_A_LAYERS, D, MLA_Q_LORA + MLA_KV_LORA + MLA_ROPE), D ** -0.5),
        'mla_g_q': 1.0 + nrm((N_A_LAYERS, MLA_Q_LORA), 0.02),
        'mla_g_kv': 1.0 + nrm((N_A_LAYERS, MLA_KV_LORA), 0.02),
        'mla_w_uq': nrm((N_A_LAYERS, MLA_Q_LORA, MLA_HEADS * (MLA_NOPE + MLA_ROPE)), MLA_Q_LORA ** -0.5),
        'mla_w_ukv': nrm((N_A_LAYERS, MLA_KV_LORA, MLA_HEADS * (MLA_NOPE + MLA_V)), MLA_KV_LORA ** -0.5),
        'mla_w_o': nrm((N_A_LAYERS, MLA_HEADS * MLA_V, D), (MLA_HEADS * MLA_V) ** -0.5),
        'rw_mu': uni((N_B_LAYERS, 6, D), 0.0, 1.0),
        'rw_w_rkv': nrm((N_B_LAYERS, 3, D, D), D ** -0.5),
        'rw_w0': uni((N_B_LAYERS, 2, D), -5.0, -0.5),
        'rw_w1': nrm((N_B_LAYERS, 2, D, RW_DECAY_LORA), 0.1 * D ** -0.5),
        'rw_w2': nrm((N_B_LAYERS, 2, RW_DECAY_LORA, D), 0.5 * RW_DECAY_LORA ** -0.5),
        'rw_a0': nrm((N_B_LAYERS, 2, D), 0.1),
        'rw_a1': nrm((N_B_LAYERS, 2, D, RW_AAA_LORA), 0.1 * D ** -0.5),
        'rw_a2': nrm((N_B_LAYERS, 2, RW_AAA_LORA, D), 0.5 * RW_AAA_LORA ** -0.5),
        'rw_v0': 1.0 + nrm((nv, D), 0.1),
        'rw_v1': nrm((nv, D, RW_MV_LORA), 0.1 * D ** -0.5),
        'rw_v2': nrm((nv, RW_MV_LORA, D), 0.5 * RW_MV_LORA ** -0.5),
        'rw_g1': nrm((N_B_LAYERS, D, RW_GATE_LORA), D ** -0.5),
        'rw_g2': nrm((N_B_LAYERS, RW_GATE_LORA, D), RW_GATE_LORA ** -0.5),
        'rw_k_k': 0.85 + nrm((N_B_LAYERS, D), 0.02),
        'rw_k_a': 1.0 + nrm((N_B_LAYERS, D), 0.02),
        'rw_r_k': nrm((N_B_LAYERS, RW_HEADS, RW_HEAD), 0.1),
        'rw_lnx_g': 1.0 + nrm((N_B_LAYERS, D), 0.02),
        'rw_lnx_b': nrm((N_B_LAYERS, D), 0.02),
        'rw_w_o': nrm((N_B_LAYERS, D, D), D ** -0.5),
        'moe_router': nrm((DEPTH, D, N_EXPERTS), D ** -0.5),
        'moe_w1': nrm((DEPTH, N_EXPERTS, D, EXPERT_FF), D ** -0.5),
        'moe_w3': nrm((DEPTH, N_EXPERTS, D, EXPERT_FF), D ** -0.5),
        'moe_w2': nrm((DEPTH, N_EXPERTS, EXPERT_FF, D), EXPERT_FF ** -0.5),
    }


def reference(x, c, ctx, c_ctx, mod_w, mod_b, norm1_g, norm2_g, final_g,
              mla_w_in, mla_g_q, mla_g_kv, mla_w_uq, mla_w_ukv, mla_w_o,
              rw_mu, rw_w_rkv, rw_w0, rw_w1, rw_w2, rw_a0, rw_a1, rw_a2,
              rw_v0, rw_v1, rw_v2, rw_g1, rw_g2, rw_k_k, rw_k_a, rw_r_k,
              rw_lnx_g, rw_lnx_b, rw_w_o,
              moe_router, moe_w1, moe_w3, moe_w2):
    rows = x.shape[1] // GRID_W
    rope = axial_rope_tables(rows)
    sc_lat = jax.nn.silu(c)
    sc_ctx = jax.nn.silu(c_ctx)
    x_lat, x_ctx = x, ctx
    v_first = None
    for layer in range(DEPTH):
        need_ctx = layer < DEPTH - 1
        m_l = (sc_lat @ mod_w[layer] + mod_b[layer])[:, None, :]
        m_c = (sc_ctx @ mod_w[layer] + mod_b[layer])[None, None, :]
        sh1_l, sc1_l, gt1_l, sh2_l, sc2_l, gt2_l = jnp.split(m_l, 6, axis=-1)
        sh1_c, sc1_c, gt1_c, sh2_c, sc2_c, gt2_c = jnp.split(m_c, 6, axis=-1)
        h_lat = rmsnorm(x_lat, norm1_g[layer]) * (1.0 + sc1_l) + sh1_l
        h_ctx = rmsnorm(x_ctx, norm1_g[layer]) * (1.0 + sc1_c) + sh1_c
        j = layer // N_MIXERS
        if layer % N_MIXERS == 0:
            y_lat, y_ctx = mla_mixer(h_lat, h_ctx, mla_w_in[j], mla_g_q[j], mla_g_kv[j],
                                     mla_w_uq[j], mla_w_ukv[j], mla_w_o[j], rope, need_ctx)
        else:
            vres = None if j == 0 else (rw_v0[j - 1], rw_v1[j - 1], rw_v2[j - 1])
            y_lat, y_ctx, v_pair = rwkv_mixer(h_lat, h_ctx, rw_mu[j], rw_w_rkv[j], rw_w0[j], rw_w1[j], rw_w2[j],
                                              rw_a0[j], rw_a1[j], rw_a2[j], rw_g1[j], rw_g2[j], rw_k_k[j],
                                              rw_k_a[j], rw_r_k[j], rw_lnx_g[j], rw_lnx_b[j], rw_w_o[j],
                                              vres, v_first, need_ctx)
            if j == 0:
                v_first = v_pair
        x_lat = x_lat + gt1_l * y_lat
        b_lat = rmsnorm(x_lat, norm2_g[layer]) * (1.0 + sc2_l) + sh2_l
        x_lat = x_lat + gt2_l * expert_choice_moe(b_lat, moe_router[layer], moe_w1[layer], moe_w3[layer], moe_w2[layer])
        if need_ctx:
            x_ctx = x_ctx + gt1_c * y_ctx
            b_ctx = rmsnorm(x_ctx, norm2_g[layer]) * (1.0 + sc2_c) + sh2_c
            x_ctx = x_ctx + gt2_c * expert_choice_moe(b_ctx, moe_router[layer], moe_w1[layer], moe_w3[layer], moe_w2[layer])
    return rmsnorm(x_lat, final_g)
```

```python
import functools
import math

import jax
import jax.numpy as jnp
from jax import lax
from jax.experimental import pallas as pl
from jax.experimental.pallas import tpu as pltpu

F32, BF16, I32 = jnp.float32, jnp.bfloat16, jnp.int32

GRID_W = 64
MLA_HEADS, MLA_NOPE, MLA_ROPE, MLA_V = 8, 128, 64, 128
MLA_Q_LORA, MLA_KV_LORA = 256, 128
MLA_SCALE = (MLA_NOPE + MLA_ROPE) ** -0.5
ROPE_THETA = 10000.0
RW_HEAD = 64
RW_LNX_EPS = 64e-5
N_EXPERTS = 16
CAPACITY_FACTOR = 2
RMS_EPS = 1e-6

LANE = 128
SUBLANE = 8
TOK_TILE = 256
SCAN_CHUNK = 64
VMEM_LIMIT = 56 * 1024 * 1024


def _cparams(*sem):
    return pltpu.CompilerParams(dimension_semantics=sem, vmem_limit_bytes=VMEM_LIMIT)


def _bdot(a, b):
    return jnp.dot(a.astype(BF16), b.astype(BF16), preferred_element_type=F32)


_NT = (((1,), (1,)), ((), ()))
_TN = (((0,), (0,)), ((), ()))
_NN = (((1,), (0,)), ((), ()))


def _split2(x):
    hi = x.astype(BF16)
    lo = (x - hi.astype(F32)).astype(BF16)
    return hi, lo


def _split3(x):
    hi = x.astype(BF16)
    r = x - hi.astype(F32)
    mid = r.astype(BF16)
    lo = (r - mid.astype(F32)).astype(BF16)
    return hi, mid, lo


def _dot3(a, b, dims=_NN):
    ah, al = _split2(a)
    bh, bl = _split2(b)
    d = functools.partial(lax.dot_general, dimension_numbers=dims, preferred_element_type=F32)
    return d(ah, bh) + (d(ah, bl) + d(al, bh))


def _dot_exact_rhs01(x, m01):
    h, m, l = _split3(x)
    d = functools.partial(jnp.dot, preferred_element_type=F32)
    return d(h, m01) + (d(m, m01) + d(l, m01))


def _rms(x, g):
    return (x * lax.rsqrt(jnp.mean(x * x, axis=-1, keepdims=True) + RMS_EPS)) * g


def _norm_mod(x, g, scale, shift):
    return _rms(x, g) * (1.0 + scale) + shift


def _sigmoid(x):
    return 1.0 / (1.0 + jnp.exp(-x))


def _mod_kernel(c_ref, w_ref, b_ref, o_ref):
    s = c_ref[...]
    s = s * _sigmoid(s)
    o_ref[...] = _bdot(s, w_ref[...]) + b_ref[...]


def _modulation(cc, mod_w, mod_b):
    depth, d, n = mod_w.shape
    tn = n // 4
    return pl.pallas_call(
        _mod_kernel,
        grid=(depth, n // tn),
        in_specs=[pl.BlockSpec((SUBLANE, d), lambda l, j: (0, 0)),
                  pl.BlockSpec((None, d, tn), lambda l, j: (l, 0, j)),
                  pl.BlockSpec((None, 1, tn), lambda l, j: (l, 0, j))],
        out_specs=pl.BlockSpec((None, SUBLANE, tn), lambda l, j: (l, 0, j)),
        out_shape=jax.ShapeDtypeStruct((depth, SUBLANE, n), F32),
        compiler_params=_cparams("arbitrary", "arbitrary"),
        name="modulation",
    )(cc, mod_w, mod_b.reshape(depth, 1, n))


def _rope128(x, cosf, sins):
    lane = lax.broadcasted_iota(I32, x.shape, 1)
    first = (lane % MLA_ROPE) < (MLA_ROPE // 2)
    rot = jnp.where(first, pltpu.roll(x, LANE - MLA_ROPE // 2, 1), pltpu.roll(x, MLA_ROPE // 2, 1))
    return x * cosf + rot * sins


def _mla_proj_kernel(x_ref, mod_ref, g1_ref, win_ref, gq_ref, gkv_ref, wuq_ref, wukv_ref,
                     cos_ref, sin_ref, q_ref, k_ref, v_ref):
    mod = mod_ref[...]
    h = _norm_mod(x_ref[...], g1_ref[...], mod[1:2], mod[0:1])
    z = _bdot(h, win_ref[...])
    cq = _rms(z[:, :MLA_Q_LORA], gq_ref[...])
    ckv = _rms(z[:, MLA_Q_LORA:MLA_Q_LORA + MLA_KV_LORA], gkv_ref[...])
    kr = z[:, MLA_Q_LORA + MLA_KV_LORA:]
    q = _bdot(cq, wuq_ref[...])
    kv = _bdot(ckv, wukv_ref[...])
    cosf, sins = cos_ref[...], sin_ref[...]
    kr_lo = _rope128(kr, cosf, sins)
    kr_hi = pltpu.roll(kr_lo, MLA_ROPE, 1)
    nope_w = MLA_HEADS * MLA_NOPE
    for g in range(MLA_HEADS // 2):
        qr = _rope128(q[:, nope_w + LANE * g:nope_w + LANE * (g + 1)], cosf, sins)
        for hh in (2 * g, 2 * g + 1):
            qn = q[:, MLA_NOPE * hh:MLA_NOPE * (hh + 1)]
            q_ref[hh] = jnp.concatenate([qn, qr], axis=1).astype(BF16)
    for hh in range(MLA_HEADS):
        base = (MLA_NOPE + MLA_V) * hh
        krh = kr_lo if hh % 2 == 0 else kr_hi
        k_ref[hh] = jnp.concatenate([kv[:, base:base + MLA_NOPE], krh], axis=1).astype(BF16)
        v_ref[hh] = kv[:, base + MLA_NOPE:base + MLA_NOPE + MLA_V].astype(BF16)


def _mla_proj(x, mod, g1, win, gq, gkv, wuq, wukv, cosf, sins, n_lat_tiles):
    b, s, d = x.shape
    tm = TOK_TILE
    const = lambda shape: pl.BlockSpec(shape, lambda i, j: (0,) * len(shape))
    hspec = lambda w: pl.BlockSpec((None, MLA_HEADS, tm, w), lambda i, j: (i, 0, j, 0))
    return pl.pallas_call(
        _mla_proj_kernel,
        grid=(b, s // tm),
        in_specs=[pl.BlockSpec((None, tm, d), lambda i, j: (i, j, 0)),
                  pl.BlockSpec((None, None, 6, d), lambda i, j: (i, (j >= n_lat_tiles).astype(I32), 0, 0)),
                  const(g1.shape), const(win.shape), const(gq.shape), const(gkv.shape),
                  const(wuq.shape), const(wukv.shape),
                  pl.BlockSpec((tm, LANE), lambda i, j: (j, 0)),
                  pl.BlockSpec((tm, LANE), lambda i, j: (j, 0))],
        out_specs=[hspec(2 * LANE), hspec(2 * LANE), hspec(MLA_V)],
        out_shape=[jax.ShapeDtypeStruct((b, MLA_HEADS, s, 2 * LANE), BF16),
                   jax.ShapeDtypeStruct((b, MLA_HEADS, s, 2 * LANE), BF16),
                   jax.ShapeDtypeStruct((b, MLA_HEADS, s, MLA_V), BF16)],
        compiler_params=_cparams("parallel", "parallel"),
        name="mla_proj",
    )(x, mod, g1, win, gq, gkv, wuq, wukv, cosf, sins)


def _attn_kernel(q_ref, k_ref, v_ref, o_ref, *, n_lat_tiles, t_lat):
    j = pl.program_id(2)
    q = q_ref[...]

    def attend(k, v):
        s = lax.dot_general(q, k, _NT, preferred_element_type=F32)
        m = jnp.max(s, axis=-1, keepdims=True)
        p = jnp.exp((s - m) * MLA_SCALE)
        l = jnp.sum(p, axis=-1, keepdims=True)
        o = jnp.dot(p.astype(BF16), v, preferred_element_type=F32)
        return (o / l).astype(o_ref.dtype)

    @pl.when(j < n_lat_tiles)
    def _():
        o_ref[...] = attend(k_ref[...], v_ref[...])

    @pl.when(j >= n_lat_tiles)
    def _():
        o_ref[...] = attend(k_ref[t_lat:, :], v_ref[t_lat:, :])


def _attention(q, k, v, t_lat):
    b, nh, s, _ = q.shape
    tq = TOK_TILE
    kern = functools.partial(_attn_kernel, n_lat_tiles=t_lat // tq, t_lat=t_lat)
    return pl.pallas_call(
        kern,
        grid=(b, nh, s // tq),
        in_specs=[pl.BlockSpec((None, None, tq, q.shape[-1]), lambda i, h, j: (i, h, j, 0)),
                  pl.BlockSpec((None, None, s, k.shape[-1]), lambda i, h, j: (i, h, 0, 0)),
                  pl.BlockSpec((None, None, s, v.shape[-1]), lambda i, h, j: (i, h, 0, 0))],
        out_specs=pl.BlockSpec((None, tq, MLA_V), lambda i, h, j: (i, j, h)),
        out_shape=jax.ShapeDtypeStruct((b, s, nh * MLA_V), BF16),
        compiler_params=_cparams("parallel", "parallel", "arbitrary"),
        name="mla_attention",
    )(q, k, v)


def _post_mixer_kernel(x_ref, y_ref, wo_ref, mod_ref, g2_ref, rt_ref, x1_ref, bm_ref, lg_ref):
    mod = mod_ref[...]
    x1 = x_ref[...] + mod[2:3] * jnp.dot(y_ref[...], wo_ref[...], preferred_element_type=F32)
    x1_ref[...] = x1
    bm = _norm_mod(x1, g2_ref[...], mod[4:5], mod[3:4]).astype(BF16)
    bm_ref[...] = bm
    lg_ref[...] = lax.dot_general(rt_ref[...], bm, _NT, preferred_element_type=F32)


def _post_mixer(x, y, wo, mod, g2, router_t, n_lat_tiles):
    b, s, d = x.shape
    tm = TOK_TILE
    const = lambda shape: pl.BlockSpec(shape, lambda i, j: (0,) * len(shape))
    return pl.pallas_call(
        _post_mixer_kernel,
        grid=(b, s // tm),
        in_specs=[pl.BlockSpec((None, tm, d), lambda i, j: (i, j, 0)),
                  pl.BlockSpec((None, tm, y.shape[-1]), lambda i, j: (i, j, 0)),
                  const(wo.shape),
                  pl.BlockSpec((None, None, 6, d), lambda i, j: (i, (j >= n_lat_tiles).astype(I32), 0, 0)),
                  const(g2.shape), const(router_t.shape)],
        out_specs=[pl.BlockSpec((None, tm, d), lambda i, j: (i, j, 0)),
                   pl.BlockSpec((None, tm, d), lambda i, j: (i, j, 0)),
                   pl.BlockSpec((None, N_EXPERTS, tm), lambda i, j: (i, 0, j))],
        out_shape=[jax.ShapeDtypeStruct((b, s, d), F32),
                   jax.ShapeDtypeStruct((b, s, d), BF16),
                   jax.ShapeDtypeStruct((b, N_EXPERTS, s), F32)],
        compiler_params=_cparams("parallel", "parallel"),
        name="post_mixer",
    )(x, y, wo, mod, g2, router_t)


def _select_kernel(lg_ref, pos_ref, aff_ref, off_ref, *, cap, tt):
    lg = lg_ref[...]
    ne, t = lg.shape
    m = jnp.max(lg, axis=0, keepdims=True)
    ex = jnp.exp(lg - m)
    aff = ex / jnp.sum(ex, axis=0, keepdims=True)
    aff_ref[...] = aff
    bits = pltpu.bitcast(aff, I32)

    def search(i, prefix):
        cand = prefix | jnp.left_shift(jnp.int32(1), 30 - i)
        cnt = jnp.sum(jnp.where(bits >= cand, 1.0, 0.0), axis=1, keepdims=True)
        return jnp.where(cnt >= cap, cand, prefix)

    thr = lax.fori_loop(0, 31, search, jnp.zeros((ne, 1), I32))
    gt = bits > thr
    eq = bits == thr
    need = cap - jnp.sum(jnp.where(gt, 1.0, 0.0), axis=1, keepdims=True)

    nch = t // LANE
    tri = (lax.broadcasted_iota(I32, (LANE, LANE), 0) <= lax.broadcasted_iota(I32, (LANE, LANE), 1))
    tri = jnp.where(tri, 1.0, 0.0).astype(BF16)

    def chunk(a, c):
        return a[:, c * LANE:(c + 1) * LANE]

    off = jnp.zeros((ne, 1), F32)
    sel = []
    for c in range(nch):
        eqc = jnp.where(chunk(eq, c), 1.0, 0.0)
        inc = jnp.dot(eqc.astype(BF16), tri, preferred_element_type=F32)
        rank = inc - eqc + off
        off = off + inc[:, LANE - 1:LANE]
        sel.append(jnp.where(chunk(gt, c) | (chunk(eq, c) & (rank < need)), 1.0, 0.0))

    lane = lax.broadcasted_iota(I32, (ne, LANE), 1)
    offs = jnp.zeros((ne, LANE), F32)
    off = jnp.zeros((ne, 1), F32)
    per_tile = tt // LANE
    for c in range(nch):
        if c % per_tile == 0:
            offs = jnp.where(lane == c // per_tile, off, offs)
        inc = jnp.dot(sel[c].astype(BF16), tri, preferred_element_type=F32)
        pos = inc - sel[c] + off
        off = off + inc[:, LANE - 1:LANE]
        pos_ref[:, c * LANE:(c + 1) * LANE] = jnp.where(sel[c] > 0.0, pos, -1.0).astype(I32)
    offs = jnp.where(lane == nch // per_tile, off, offs)
    off_ref[...] = offs.astype(I32)


def _select(logits_t, t0_blk, t, cap):
    b, ne, _ = logits_t.shape
    tt = min(TOK_TILE, t)
    kern = functools.partial(_select_kernel, cap=cap, tt=tt)
    return pl.pallas_call(
        kern,
        grid=(b,),
        in_specs=[pl.BlockSpec((None, ne, t), lambda i: (i, 0, t0_blk))],
        out_specs=[pl.BlockSpec((None, ne, t), lambda i: (i, 0, 0)),
                   pl.BlockSpec((None, ne, t), lambda i: (i, 0, 0)),
                   pl.BlockSpec((None, ne, LANE), lambda i: (i, 0, 0))],
        out_shape=[jax.ShapeDtypeStruct((b, ne, t), I32),
                   jax.ShapeDtypeStruct((b, ne, t), F32),
                   jax.ShapeDtypeStruct((b, ne, LANE), I32)],
        compiler_params=_cparams("parallel"),
        name="moe_select",
    )(logits_t)


def _window_onehot(posr, lo, cap, w, rows):
    base = pl.multiple_of(jnp.minimum((lo // SUBLANE) * SUBLANE, cap - w), SUBLANE)
    hit = ((posr - base) == rows) & (posr >= lo)
    return base, hit


def _gather_kernel(off_sm, h_ref, pos_ref, aff_ref, xe_ref, gs_ref, acc, gacc, *, cap, tt, nt, w):
    bi, e = pl.program_id(0), pl.program_id(1)
    ne = pl.num_programs(1)
    acc[...] = jnp.zeros_like(acc)
    gacc[...] = jnp.zeros_like(gacc)
    rows = lax.broadcasted_iota(I32, (w, tt), 0)
    obase = (bi * ne + e) * (nt + 1)

    def tile(j, carry):
        p0, p1 = off_sm[obase + j], off_sm[obase + j + 1]
        posr = pos_ref[pl.ds(j, 1), :]
        affr = aff_ref[pl.ds(j, 1), :]
        ht = h_ref[pl.ds(pl.multiple_of(j * tt, tt), tt), :]

        def window(lo):
            base, hit = _window_onehot(posr, lo, cap, w, rows)
            onehot = jnp.where(hit, 1.0, 0.0).astype(BF16)
            acc[pl.ds(base, w), :] += jnp.dot(onehot, ht, preferred_element_type=F32)
            gacc[pl.ds(base, w), :] += jnp.sum(jnp.where(hit, affr, 0.0), axis=1, keepdims=True)
            return base + w

        nxt = window(p0)

        @pl.when(p1 > nxt)
        def _():
            window(nxt)

        return carry

    lax.fori_loop(0, nt, tile, 0)
    xe_ref[...] = acc[...].astype(BF16)
    gs_ref[...] = gacc[...]


def _gather(off_flat, bm, pos4, aff4, t0_blk, t, cap):
    b, _, d = bm.shape
    ne, nt, tt = pos4.shape[1], pos4.shape[2], pos4.shape[3]
    w = min(tt, cap)
    kern = functools.partial(_gather_kernel, cap=cap, tt=tt, nt=nt, w=w)
    grid_spec = pltpu.PrefetchScalarGridSpec(
        num_scalar_prefetch=1,
        grid=(b, ne),
        in_specs=[pl.BlockSpec((None, t, d), lambda i, e, o: (i, t0_blk, 0)),
                  pl.BlockSpec((None, None, nt, tt), lambda i, e, o: (i, e, 0, 0)),
                  pl.BlockSpec((None, None, nt, tt), lambda i, e, o: (i, e, 0, 0))],
        out_specs=[pl.BlockSpec((None, None, cap, d), lambda i, e, o: (e, i, 0, 0)),
                   pl.BlockSpec((None, None, cap, 1), lambda i, e, o: (e, i, 0, 0))],
        scratch_shapes=[pltpu.VMEM((cap, d), F32), pltpu.VMEM((cap, 1), F32)])
    return pl.pallas_call(
        kern,
        grid_spec=grid_spec,
        out_shape=[jax.ShapeDtypeStruct((ne, b, cap, d), BF16),
                   jax.ShapeDtypeStruct((ne, b, cap, 1), F32)],
        compiler_params=_cparams("parallel", "arbitrary"),
        name="moe_gather",
    )(off_flat, bm, pos4, aff4)


def _ffn_kernel(x_ref, gs_ref, w1_ref, w3_ref, w2_ref, o_ref):
    x = x_ref[...]
    h1 = jnp.dot(x, w1_ref[...], preferred_element_type=F32)
    h3 = jnp.dot(x, w3_ref[...], preferred_element_type=F32)
    hid = (h1 * _sigmoid(h1)) * h3
    o_ref[...] = jnp.dot(hid.astype(BF16), w2_ref[...], preferred_element_type=F32) * gs_ref[...]


def _ffn(xe, gs, w1, w3, w2):
    ne, r, d = xe.shape
    f = w1.shape[-1]
    tr = min(r, 512)
    return pl.pallas_call(
        _ffn_kernel,
        grid=(ne, r // tr),
        in_specs=[pl.BlockSpec((None, tr, d), lambda e, i: (e, i, 0)),
                  pl.BlockSpec((None, tr, 1), lambda e, i: (e, i, 0)),
                  pl.BlockSpec((None, d, f), lambda e, i: (e, 0, 0)),
                  pl.BlockSpec((None, d, f), lambda e, i: (e, 0, 0)),
                  pl.BlockSpec((None, f, d), lambda e, i: (e, 0, 0))],
        out_specs=pl.BlockSpec((None, tr, d), lambda e, i: (e, i, 0)),
        out_shape=jax.ShapeDtypeStruct((ne, r, d), F32),
        compiler_params=_cparams("parallel", "arbitrary"),
        name="moe_ffn",
    )(xe, gs, w1, w3, w2)


def _combine_kernel(off_sm, x1_ref, mod_ref, pos_ref, ye_ref, o_ref, *, cap, tt, nt, w, tiles_per_blk):
    bi, ci, e = pl.program_id(0), pl.program_id(1), pl.program_id(2)
    ne = pl.num_programs(2)

    @pl.when(e == 0)
    def _():
        o_ref[...] = jnp.zeros_like(o_ref)

    rows = lax.broadcasted_iota(I32, (w, tt), 0)
    obase = (bi * ne + e) * (nt + 1)

    def tile(jj, carry):
        j = ci * tiles_per_blk + jj
        p0, p1 = off_sm[obase + j], off_sm[obase + j + 1]
        posr = pos_ref[pl.ds(j, 1), :]
        r0 = pl.multiple_of(jj * tt, tt)

        def window(lo):
            base, hit = _window_onehot(posr, lo, cap, w, rows)
            onehot = jnp.where(hit, 1.0, 0.0).astype(BF16)
            hi, lo_part = _split2(ye_ref[pl.ds(base, w), :])
            d = functools.partial(lax.dot_general, dimension_numbers=_TN, preferred_element_type=F32)
            o_ref[pl.ds(r0, tt), :] += d(onehot, hi) + d(onehot, lo_part)
            return base + w

        nxt = window(p0)

        @pl.when(p1 > nxt)
        def _():
            window(nxt)

        return carry

    lax.fori_loop(0, tiles_per_blk, tile, 0)

    @pl.when(e == ne - 1)
    def _():
        o_ref[...] = x1_ref[...] + mod_ref[...][5:6] * o_ref[...]


def _combine(off_flat, x1, mod, stream, pos4, ye, t0_blk, t, cap):
    b, s, d = x1.shape
    ne, nt, tt = pos4.shape[1], pos4.shape[2], pos4.shape[3]
    w = min(tt, cap)
    tc = min(t, 2048)
    nblk = t // tc
    kern = functools.partial(_combine_kernel, cap=cap, tt=tt, nt=nt, w=w, tiles_per_blk=tc // tt)
    grid_spec = pltpu.PrefetchScalarGridSpec(
        num_scalar_prefetch=1,
        grid=(b, nblk, ne),
        in_specs=[pl.BlockSpec((None, tc, d), lambda i, c, e, o: (i, t0_blk * nblk + c, 0)),
                  pl.BlockSpec((None, None, 6, d), lambda i, c, e, o: (i, stream, 0, 0)),
                  pl.BlockSpec((None, None, nt, tt), lambda i, c, e, o: (i, e, 0, 0)),
                  pl.BlockSpec((None, None, cap, d), lambda i, c, e, o: (e, i, 0, 0))],
        out_specs=pl.BlockSpec((None, tc, d), lambda i, c, e, o: (i, t0_blk * nblk + c, 0)))
    return pl.pallas_call(
        kern,
        grid_spec=grid_spec,
        out_shape=jax.ShapeDtypeStruct((b, s, d), F32),
        input_output_aliases={1: 0},
        compiler_params=_cparams("parallel", "parallel", "arbitrary"),
        name="moe_combine",
    )(off_flat, x1, mod, pos4, ye)


def _moe_stream(x1, bm, logits_t, mod, stream, t0_blk, t, w1, w3, w2):
    b = x1.shape[0]
    cap = CAPACITY_FACTOR * t // N_EXPERTS
    pos, aff, offs = _select(logits_t, t0_blk, t, cap)
    tt = min(TOK_TILE, t)
    nt = t // tt
    pos4 = pos.reshape(b, N_EXPERTS, nt, tt)
    aff4 = aff.reshape(b, N_EXPERTS, nt, tt)
    off_flat = offs[:, :, :nt + 1].reshape(-1)
    xe, gs = _gather(off_flat, bm, pos4, aff4, t0_blk, t, cap)
    d = x1.shape[-1]
    ye = _ffn(xe.reshape(N_EXPERTS, b * cap, d), gs.reshape(N_EXPERTS, b * cap, 1), w1, w3, w2)
    return _combine(off_flat, x1, mod, stream, pos4, ye.reshape(N_EXPERTS, b, cap, d), t0_blk, t, cap)


def _rwkv_stream_kernel(*refs, t_lat, has_vres):
    if has_vres:
        (x_ref, xp_ref, xn_ref, mod_ref, g1_ref, mu_ref, wr_ref, wk_ref, wv_ref, w1_ref, w2_ref, w0_ref,
         a1_ref, a2_ref, a0_ref, gg1_ref, gg2_ref, kk_ref, ka_ref, rk_ref, seg_ref, segt_ref,
         vf_ref, v0_ref, v1_ref, v2_ref,
         r_out, v_out, kk_out, lw_out, kd_out, b_out, g_out, bonus_out) = refs
    else:
        (x_ref, xp_ref, xn_ref, mod_ref, g1_ref, mu_ref, wr_ref, wk_ref, wv_ref, w1_ref, w2_ref, w0_ref,
         a1_ref, a2_ref, a0_ref, gg1_ref, gg2_ref, kk_ref, ka_ref, rk_ref, seg_ref, segt_ref,
         r_out, v_out, kk_out, lw_out, kd_out, b_out, g_out, bonus_out) = refs
    j = pl.program_id(1)
    mod = mod_ref[...]
    g1 = g1_ref[...]
    nm = lambda x: _norm_mod(x, g1, mod[1:2], mod[0:1])
    h = nm(x_ref[...])
    tm, d = h.shape
    hp = nm(xp_ref[...])[SUBLANE - 1:SUBLANE]
    hn = nm(xn_ref[...])[0:1]
    row = lax.broadcasted_iota(I32, (tm, 1), 0)
    grow = row + j * tm
    s_tot = pl.num_programs(1) * tm
    first = (grow == 0) | (grow == t_lat)
    last = (grow == t_lat - 1) | (grow == s_tot - 1)
    prev = jnp.where(row == 0, hp, pltpu.roll(h, 1, 0))
    prev = jnp.where(first, 0.0, prev)
    nxt = jnp.where(row == tm - 1, hn, pltpu.roll(h, tm - 1, 0))
    nxt = jnp.where(last, 0.0, nxt)
    xx = 0.5 * (prev + nxt) - h
    mu = mu_ref[...]
    xs = [h + xx * mu[m:m + 1] for m in range(6)]
    r = _bdot(xs[0], wr_ref[...])
    k = _bdot(xs[1], wk_ref[...])
    v = _bdot(xs[2], wv_ref[...])
    if has_vres:
        gate = _sigmoid(v0_ref[...] + _bdot(_bdot(xs[2], v1_ref[...]), v2_ref[...]))
        v = v + (vf_ref[...] - v) * gate
    wz = w0_ref[...] + _bdot(jnp.tanh(_bdot(xs[3], w1_ref[...])), w2_ref[...])
    lw = -math.exp(-0.5) * _sigmoid(wz)
    a = _sigmoid(a0_ref[...] + _bdot(_bdot(xs[4], a1_ref[...]), a2_ref[...]))
    g = _bdot(_sigmoid(_bdot(xs[5], gg1_ref[...])), gg2_ref[...])
    seg, segt = seg_ref[...], segt_ref[...]
    segsum = lambda t: _dot_exact_rhs01(_dot_exact_rhs01(t, seg), segt)
    kkr = k * kk_ref[...]
    kk = kkr / jnp.maximum(jnp.sqrt(segsum(kkr * kkr)), 1e-12)
    ka = ka_ref[...]
    kd0 = k * (1.0 + (a[:, :d] - 1.0) * ka)
    kd1 = k * (1.0 + (a[:, d:] - 1.0) * ka)
    r_out[...] = r
    v_out[...] = v
    kk_out[...] = kk
    lw_out[...] = lw
    kd_out[:, :d] = kd0
    kd_out[:, d:] = kd1
    b_out[:, :d] = kk * a[:, :d]
    b_out[:, d:] = kk * a[:, d:]
    g_out[...] = g
    bonus_out[...] = segsum(r * (kd0 + kd1) * rk_ref[...]) * v


def _rwkv_stream(x, mod, g1, wts, vres, t_lat):
    b, s, d = x.shape
    tm = TOK_TILE
    nlt = t_lat // tm
    nsub = tm // SUBLANE
    const = lambda a: pl.BlockSpec(a.shape, lambda i, j: (0,) * a.ndim)
    tok = lambda w: pl.BlockSpec((None, tm, w), lambda i, j: (i, j, 0))
    in_specs = [tok(d),
                pl.BlockSpec((None, SUBLANE, d), lambda i, j: (i, jnp.maximum(j * nsub - 1, 0), 0)),
                pl.BlockSpec((None, SUBLANE, d), lambda i, j: (i, jnp.minimum((j + 1) * nsub, s // SUBLANE - 1), 0)),
                pl.BlockSpec((None, None, 6, d), lambda i, j: (i, (j >= nlt).astype(I32), 0, 0)),
                const(g1)] + [const(a) for a in wts]
    args = [x, x, x, mod, g1] + list(wts)
    if vres is not None:
        vf, v0, v1, v2 = vres
        in_specs += [tok(d), const(v0), const(v1), const(v2)]
        args += [vf, v0, v1, v2]
    kern = functools.partial(_rwkv_stream_kernel, t_lat=t_lat, has_vres=vres is not None)
    widths = [d, d, d, 2 * d, 2 * d, 2 * d, d, d]
    return pl.pallas_call(
        kern,
        grid=(b, s // tm),
        in_specs=in_specs,
        out_specs=[tok(w) for w in widths],
        out_shape=[jax.ShapeDtypeStruct((b, s, w), F32) for w in widths],
        compiler_params=_cparams("parallel", "parallel"),
        name="rwkv_stream",
    )(*args)


def _scan_kernel(r_ref, v_ref, kk_ref, lw_ref, kd_ref, b_ref, y_ref, h_ref, *, reverse, nchunk):
    L = SCAN_CHUNK

    @pl.when(pl.program_id(2) == 0)
    def _():
        h_ref[...] = jnp.zeros_like(h_ref)

    n2 = 2 * L
    rr = lax.broadcasted_iota(I32, (n2, n2), 0)
    cc = lax.broadcasted_iota(I32, (n2, n2), 1)
    tr, tc = rr % L, cc % L
    same = (rr // L) == (cc // L)
    before = (tr < tc) if reverse else (tr > tc)
    strict = same & before
    incl = same & (before | (tr == tc))
    eye = rr == cc
    blk16 = (rr // 16) == (cc // 16)
    blk32 = (rr // 32) == (cc // 32)
    ri = lax.broadcasted_iota(I32, (L, L), 0)
    ci = lax.broadcasted_iota(I32, (L, L), 1)
    tri = jnp.where((ci >= ri) if reverse else (ci <= ri), 1.0, 0.0).astype(BF16)
    head0 = lax.broadcasted_iota(I32, (L, LANE), 1) < RW_HEAD

    def stack(x):
        return jnp.concatenate([jnp.where(head0, x, 0.0), jnp.where(head0, 0.0, x)], axis=0)

    order = range(nchunk - 1, -1, -1) if reverse else range(nchunk)
    for ch in order:
        sl = pl.ds(ch * L, L)
        lw = lw_ref[sl, :]
        hi, mid, lo = _split3(lw)
        dd = functools.partial(jnp.dot, preferred_element_type=F32)
        cs = dd(tri, hi) + (dd(tri, mid) + dd(tri, lo))
        total = cs[0:1] if reverse else cs[L - 1:L]
        gam = jnp.exp(cs)
        ginv = jnp.exp(-cs)
        gprev = jnp.exp(cs - lw)
        gend = jnp.exp(total - cs)
        kk, bb, kd, r, v = kk_ref[sl, :], b_ref[sl, :], kd_ref[sl, :], r_ref[sl, :], v_ref[sl, :]
        kkm, btm, ktm, rtm = stack(kk * gprev), stack(bb * ginv), stack(kd * ginv), stack(r * gam)
        bhm, khm, vm = stack(bb * gend), stack(kd * gend), stack(v)
        n = jnp.where(strict, _dot3(kkm, btm, _NT), 0.0)
        m = jnp.where(strict, _dot3(kkm, ktm, _NT), 0.0)
        ab = jnp.where(incl, _dot3(rtm, btm, _NT), 0.0)
        ak = jnp.where(incl, _dot3(rtm, ktm, _NT), 0.0)
        nd = jnp.where(blk16, n, 0.0)
        x = jnp.where(eye, 1.0, 0.0) - nd
        pw = nd
        for _ in range(3):
            pw = _dot3(pw, pw)
            x = x + _dot3(x, pw)
        for inner, outer in ((blk16, blk32), (blk32, same)):
            c = jnp.where(outer & jnp.logical_not(inner), n, 0.0)
            x = x - _dot3(_dot3(x, c), x)
        mv = _dot3(m, vm)
        w = _dot3(x, kkm)
        u0 = _dot3(x, mv)
        h0 = h_ref[...]
        p = jnp.where(eye, jnp.exp(total), 0.0) - _dot3(bhm, w, _TN)
        q = _dot3(khm, vm, _TN) - _dot3(bhm, u0, _TN)
        rres = rtm - _dot3(ab, w)
        y0 = _dot3(ak, vm) - _dot3(ab, u0)
        y = _dot3(rres, h0) + y0
        y_ref[sl, :] = y[:L] + y[L:]
        h_ref[...] = _dot3(p, h0) + q


def _scan(r, v, kk, lw, kd, bb, t_lat, reverse):
    b, s, d = r.shape
    tb = TOK_TILE
    nl, nc = t_lat // tb, (s - t_lat) // tb
    npair = d // LANE
    z = 1 if reverse else 0

    def blk(c):
        if reverse:
            return jnp.where(c < nc, nl + nc - 1 - c, nl - 1 - (c - nc))
        return jnp.where(c < nc, nl + c, c - nc)

    shared = pl.BlockSpec((None, tb, LANE), lambda i, p, c: (i, blk(c), p))
    dirn = pl.BlockSpec((None, tb, LANE), lambda i, p, c: (i, blk(c), z * npair + p))
    kern = functools.partial(_scan_kernel, reverse=reverse, nchunk=tb // SCAN_CHUNK)
    return pl.pallas_call(
        kern,
        grid=(b, npair, nl + nc),
        in_specs=[shared, shared, shared, dirn, dirn, dirn],
        out_specs=pl.BlockSpec((None, tb, LANE), lambda i, p, c: (i, blk(c), p)),
        out_shape=jax.ShapeDtypeStruct((b, s, d), F32),
        scratch_shapes=[pltpu.VMEM((LANE, LANE), F32)],
        compiler_params=_cparams("parallel", "parallel", "arbitrary"),
        name="wkv_scan_bwd" if reverse else "wkv_scan_fwd",
    )(r, v, kk, lw, kd, bb)


def _rwkv_out_kernel(yf_ref, yb_ref, bonus_ref, g_ref, lg_ref, lb_ref, seg_ref, segt_ref, o_ref):
    seg, segt = seg_ref[...], segt_ref[...]
    segmean = lambda t: _dot_exact_rhs01(_dot_exact_rhs01(t, seg), segt) * (1.0 / RW_HEAD)
    y = yf_ref[...] + yb_ref[...]
    dlt = y - segmean(y)
    var = segmean(dlt * dlt)
    yn = (dlt * lax.rsqrt(var + RW_LNX_EPS)) * lg_ref[...] + lb_ref[...]
    o_ref[...] = ((yn + bonus_ref[...]) * g_ref[...]).astype(BF16)


def _rwkv_out(yf, yb, bonus, g, lnx_g, lnx_b, seg, segt):
    b, s, d = yf.shape
    tm = TOK_TILE
    tok = pl.BlockSpec((None, tm, d), lambda i, j: (i, j, 0))
    const = lambda a: pl.BlockSpec(a.shape, lambda i, j: (0,) * a.ndim)
    return pl.pallas_call(
        _rwkv_out_kernel,
        grid=(b, s // tm),
        in_specs=[tok, tok, tok, tok, const(lnx_g), const(lnx_b), const(seg), const(segt)],
        out_specs=tok,
        out_shape=jax.ShapeDtypeStruct((b, s, d), BF16),
        compiler_params=_cparams("parallel", "parallel"),
        name="rwkv_out",
    )(yf, yb, bonus, g, lnx_g, lnx_b, seg, segt)


def _final_kernel(x_ref, g_ref, o_ref):
    o_ref[...] = _rms(x_ref[...], g_ref[...])


def _final_norm(x, g, t_lat):
    b, _, d = x.shape
    tm = TOK_TILE
    return pl.pallas_call(
        _final_kernel,
        grid=(b, t_lat // tm),
        in_specs=[pl.BlockSpec((None, tm, d), lambda i, j: (i, j, 0)),
                  pl.BlockSpec(g.shape, lambda i, j: (0, 0))],
        out_specs=pl.BlockSpec((None, tm, d), lambda i, j: (i, j, 0)),
        out_shape=jax.ShapeDtypeStruct((b, t_lat, d), F32),
        compiler_params=_cparams("parallel", "parallel"),
        name="final_norm",
    )(x, g)


def _rope_tables(t_lat, t_ctx):
    rows = t_lat // GRID_W
    r = jnp.broadcast_to(jnp.arange(rows)[:, None], (rows, GRID_W)).reshape(-1).astype(F32)
    col = jnp.broadcast_to(jnp.arange(GRID_W)[None, :], (rows, GRID_W)).reshape(-1).astype(F32)
    n_freq = MLA_ROPE // 4
    inv = ROPE_THETA ** (-jnp.arange(n_freq, dtype=F32) / n_freq)
    ang = jnp.concatenate([r[:, None] * inv, col[:, None] * inv], axis=-1)
    cos, sin = jnp.cos(ang), jnp.sin(ang)
    cosf = jnp.concatenate([cos, cos, cos, cos], axis=-1)
    sins = jnp.concatenate([-sin, sin, -sin, sin], axis=-1)
    cosf = jnp.concatenate([cosf, jnp.ones((t_ctx, LANE), F32)], axis=0)
    sins = jnp.concatenate([sins, jnp.zeros((t_ctx, LANE), F32)], axis=0)
    return cosf, sins


def _blockdiag2(w):
    z = jnp.zeros_like(w[0])
    return jnp.concatenate([jnp.concatenate([w[0], z], axis=1), jnp.concatenate([z, w[1]], axis=1)], axis=0)


def kernel(x, c, ctx, c_ctx, mod_w, mod_b, norm1_g, norm2_g, final_g, mla_w_in, mla_g_q, mla_g_kv, mla_w_uq, mla_w_ukv, mla_w_o, rw_mu, rw_w_rkv, rw_w0, rw_w1, rw_w2, rw_a0, rw_a1, rw_a2, rw_v0, rw_v1, rw_v2, rw_g1, rw_g2, rw_k_k, rw_k_a, rw_r_k, rw_lnx_g, rw_lnx_b, rw_w_o, moe_router, moe_w1, moe_w3, moe_w2):
    b, t_lat, d = x.shape
    t_ctx = ctx.shape[1]
    depth = mod_w.shape[0]
    assert b + 1 <= SUBLANE and t_lat % TOK_TILE == 0 and t_ctx % TOK_TILE == 0 and d % LANE == 0
    n_lat_tiles = t_lat // TOK_TILE
    ctx_blk = t_lat // t_ctx
    assert ctx_blk * t_ctx == t_lat

    cc = jnp.concatenate([c, c_ctx[None], jnp.zeros((SUBLANE - b - 1, d), F32)], axis=0)
    mods = _modulation(cc, mod_w, mod_b).reshape(depth, SUBLANE, 6, d)
    mod_all = jnp.stack([mods[:, :b], jnp.broadcast_to(mods[:, b:b + 1], (depth, b, 6, d))], axis=2)

    xs = jnp.concatenate([x, ctx], axis=1)
    cosf, sins = _rope_tables(t_lat, t_ctx)
    row = lambda a: a.reshape(1, -1)
    seg = (jnp.arange(d)[:, None] // RW_HEAD == jnp.arange(LANE)[None, :]).astype(BF16)
    segt = seg.T

    v_first = None
    for layer in range(depth):
        need_ctx = layer < depth - 1
        mod = mod_all[layer]
        j = layer // 2
        if layer % 2 == 0:
            win = jnp.pad(mla_w_in[j], ((0, 0), (0, 4 * LANE - mla_w_in.shape[-1]))).astype(BF16)
            wuq = mla_w_uq[j].reshape(MLA_Q_LORA, MLA_HEADS, MLA_NOPE + MLA_ROPE)
            wuq = jnp.concatenate([wuq[..., :MLA_NOPE].reshape(MLA_Q_LORA, -1),
                                   wuq[..., MLA_NOPE:].reshape(MLA_Q_LORA, -1)], axis=1).astype(BF16)
            q, k, v = _mla_proj(xs, mod, row(norm1_g[layer]), win, row(mla_g_q[j]), row(mla_g_kv[j]),
                                wuq, mla_w_ukv[j].astype(BF16), cosf, sins, n_lat_tiles)
            y = _attention(q, k, v, t_lat)
            wo = mla_w_o[j].astype(BF16)
        else:
            lora = lambda w: jnp.concatenate([w[0], w[1]], axis=1).astype(BF16)
            wts = [rw_mu[j], rw_w_rkv[j, 0].astype(BF16), rw_w_rkv[j, 1].astype(BF16), rw_w_rkv[j, 2].astype(BF16),
                   lora(rw_w1[j]), _blockdiag2(rw_w2[j]).astype(BF16), rw_w0[j].reshape(1, -1),
                   lora(rw_a1[j]), _blockdiag2(rw_a2[j]).astype(BF16), rw_a0[j].reshape(1, -1),
                   rw_g1[j].astype(BF16), rw_g2[j].astype(BF16),
                   row(rw_k_k[j]), row(rw_k_a[j]), row(rw_r_k[j]), seg, segt]
            vres = None
            if j > 0:
                pad = LANE - rw_v1.shape[-1]
                vres = (v_first, row(rw_v0[j - 1]), jnp.pad(rw_v1[j - 1], ((0, 0), (0, pad))).astype(BF16),
                        jnp.pad(rw_v2[j - 1], ((0, pad), (0, 0))).astype(BF16))
            r, v, kk, lw, kd, bb, g, bonus = _rwkv_stream(xs, mod, row(norm1_g[layer]), wts, vres, t_lat)
            if j == 0:
                v_first = v
            yf = _scan(r, v, kk, lw, kd, bb, t_lat, reverse=False)
            yb = _scan(r, v, kk, lw, kd, bb, t_lat, reverse=True)
            y = _rwkv_out(yf, yb, bonus, g, row(rw_lnx_g[j]), row(rw_lnx_b[j]), seg, segt)
            wo = rw_w_o[j].astype(BF16)
        x1, bm, logits_t = _post_mixer(xs, y, wo, mod, row(norm2_g[layer]), moe_router[layer].T.astype(BF16),
                                       n_lat_tiles)
        w1, w3, w2 = moe_w1[layer].astype(BF16), moe_w3[layer].astype(BF16), moe_w2[layer].astype(BF16)
        xs = _moe_stream(x1, bm, logits_t, mod, 0, 0, t_lat, w1, w3, w2)
        if need_ctx:
            xs = _moe_stream(xs, bm, logits_t, mod, 1, ctx_blk, t_ctx, w1, w3, w2)
    return _final_norm(xs, row(final_g), t_lat)
```

```python
import functools
import math

import jax
import jax.numpy as jnp
from jax import lax
from jax.experimental import pallas as pl
from jax.experimental.pallas import tpu as pltpu

F32, BF16, I32 = jnp.float32, jnp.bfloat16, jnp.int32

GRID_W = 64
MLA_HEADS, MLA_NOPE, MLA_ROPE, MLA_V = 8, 128, 64, 128
MLA_Q_LORA, MLA_KV_LORA = 256, 128
MLA_SCALE = (MLA_NOPE + MLA_ROPE) ** -0.5
ROPE_THETA = 10000.0
RW_HEAD = 64
RW_LNX_EPS = 64e-5
N_EXPERTS = 16
CAPACITY_FACTOR = 2
RMS_EPS = 1e-6

LANE = 128
SUBLANE = 8
TOK_TILE = 256
SCAN_CHUNK = 64
SCAN_PAIRS = 2
VMEM_LIMIT = 56 * 1024 * 1024


def _cparams(*sem):
    return pltpu.CompilerParams(dimension_semantics=sem, vmem_limit_bytes=VMEM_LIMIT)


def _bdot(a, b):
    return jnp.dot(a.astype(BF16), b.astype(BF16), preferred_element_type=F32)


_NT = (((1,), (1,)), ((), ()))
_TN = (((0,), (0,)), ((), ()))
_NN = (((1,), (0,)), ((), ()))


def _split2(x):
    hi = x.astype(BF16)
    lo = (x - hi.astype(F32)).astype(BF16)
    return hi, lo


def _split3(x):
    hi = x.astype(BF16)
    r = x - hi.astype(F32)
    mid = r.astype(BF16)
    lo = (r - mid.astype(F32)).astype(BF16)
    return hi, mid, lo


def _mm3(a, b, dims=_NN):
    d = functools.partial(lax.dot_general, dimension_numbers=dims, preferred_element_type=F32)
    return d(a[0], b[0]) + (d(a[0], b[1]) + d(a[1], b[0]))


def _dot_exact_rhs01(x, m01):
    h, m, l = _split3(x)
    d = functools.partial(jnp.dot, preferred_element_type=F32)
    return d(h, m01) + (d(m, m01) + d(l, m01))


def _rms(x, g):
    return (x * lax.rsqrt(jnp.mean(x * x, axis=-1, keepdims=True) + RMS_EPS)) * g


def _norm_mod(x, g, scale, shift):
    return _rms(x, g) * (1.0 + scale) + shift


def _sigmoid(x):
    return 1.0 / (1.0 + jnp.exp(-x))


def _mod_kernel(c_ref, w_ref, b_ref, o_ref):
    s = c_ref[...]
    s = s * _sigmoid(s)
    o_ref[...] = _bdot(s, w_ref[...]) + b_ref[...]


def _modulation(cc, mod_w, mod_b):
    depth, d, n = mod_w.shape
    tn = n // 4
    return pl.pallas_call(
        _mod_kernel,
        grid=(depth, n // tn),
        in_specs=[pl.BlockSpec((SUBLANE, d), lambda l, j: (0, 0)),
                  pl.BlockSpec((None, d, tn), lambda l, j: (l, 0, j)),
                  pl.BlockSpec((None, 1, tn), lambda l, j: (l, 0, j))],
        out_specs=pl.BlockSpec((None, SUBLANE, tn), lambda l, j: (l, 0, j)),
        out_shape=jax.ShapeDtypeStruct((depth, SUBLANE, n), F32),
        compiler_params=_cparams("arbitrary", "arbitrary"),
        name="modulation",
    )(cc, mod_w, mod_b.reshape(depth, 1, n))


def _rope128(x, cosf, sins):
    lane = lax.broadcasted_iota(I32, x.shape, 1)
    first = (lane % MLA_ROPE) < (MLA_ROPE // 2)
    rot = jnp.where(first, pltpu.roll(x, LANE - MLA_ROPE // 2, 1), pltpu.roll(x, MLA_ROPE // 2, 1))
    return x * cosf + rot * sins


def _mla_proj_kernel(x_ref, mod_ref, g1_ref, win_ref, gq_ref, gkv_ref, wuq_ref, wukv_ref,
                     cos_ref, sin_ref, q_ref, k_ref, v_ref):
    mod = mod_ref[...]
    h = _norm_mod(x_ref[...], g1_ref[...], mod[1:2], mod[0:1])
    z = _bdot(h, win_ref[...])
    cq = _rms(z[:, :MLA_Q_LORA], gq_ref[...])
    ckv = _rms(z[:, MLA_Q_LORA:MLA_Q_LORA + MLA_KV_LORA], gkv_ref[...])
    kr = z[:, MLA_Q_LORA + MLA_KV_LORA:]
    q = _bdot(cq, wuq_ref[...])
    kv = _bdot(ckv, wukv_ref[...])
    cosf, sins = cos_ref[...], sin_ref[...]
    kr_lo = _rope128(kr, cosf, sins)
    kr_hi = pltpu.roll(kr_lo, MLA_ROPE, 1)
    nope_w = MLA_HEADS * MLA_NOPE
    for g in range(MLA_HEADS // 2):
        qr = _rope128(q[:, nope_w + LANE * g:nope_w + LANE * (g + 1)], cosf, sins)
        for hh in (2 * g, 2 * g + 1):
            qn = q[:, MLA_NOPE * hh:MLA_NOPE * (hh + 1)]
            q_ref[hh] = jnp.concatenate([qn, qr], axis=1).astype(BF16)
    for hh in range(MLA_HEADS):
        base = (MLA_NOPE + MLA_V) * hh
        krh = kr_lo if hh % 2 == 0 else kr_hi
        k_ref[hh] = jnp.concatenate([kv[:, base:base + MLA_NOPE], krh], axis=1).astype(BF16)
        v_ref[hh] = kv[:, base + MLA_NOPE:base + MLA_NOPE + MLA_V].astype(BF16)


def _mla_proj(x, mod, g1, win, gq, gkv, wuq, wukv, cosf, sins, n_lat_tiles):
    b, s, d = x.shape
    tm = TOK_TILE
    const = lambda shape: pl.BlockSpec(shape, lambda i, j: (0,) * len(shape))
    hspec = lambda w: pl.BlockSpec((None, MLA_HEADS, tm, w), lambda i, j: (i, 0, j, 0))
    return pl.pallas_call(
        _mla_proj_kernel,
        grid=(b, s // tm),
        in_specs=[pl.BlockSpec((None, tm, d), lambda i, j: (i, j, 0)),
                  pl.BlockSpec((None, None, 6, d), lambda i, j: (i, (j >= n_lat_tiles).astype(I32), 0, 0)),
                  const(g1.shape), const(win.shape), const(gq.shape), const(gkv.shape),
                  const(wuq.shape), const(wukv.shape),
                  pl.BlockSpec((tm, LANE), lambda i, j: (j, 0)),
                  pl.BlockSpec((tm, LANE), lambda i, j: (j, 0))],
        out_specs=[hspec(2 * LANE), hspec(2 * LANE), hspec(MLA_V)],
        out_shape=[jax.ShapeDtypeStruct((b, MLA_HEADS, s, 2 * LANE), BF16),
                   jax.ShapeDtypeStruct((b, MLA_HEADS, s, 2 * LANE), BF16),
                   jax.ShapeDtypeStruct((b, MLA_HEADS, s, MLA_V), BF16)],
        compiler_params=_cparams("parallel", "parallel"),
        name="mla_proj",
    )(x, mod, g1, win, gq, gkv, wuq, wukv, cosf, sins)


def _attn_kernel(q_ref, k_ref, v_ref, o_ref, *, n_lat_tiles, t_lat):
    j = pl.program_id(2)
    q = q_ref[...]

    def attend(k, v):
        s = lax.dot_general(q, k, _NT, preferred_element_type=F32)
        m = jnp.max(s, axis=-1, keepdims=True)
        p = jnp.exp((s - m) * MLA_SCALE)
        l = jnp.sum(p, axis=-1, keepdims=True)
        o = jnp.dot(p.astype(BF16), v, preferred_element_type=F32)
        return (o / l).astype(o_ref.dtype)

    @pl.when(j < n_lat_tiles)
    def _():
        o_ref[...] = attend(k_ref[...], v_ref[...])

    @pl.when(j >= n_lat_tiles)
    def _():
        o_ref[...] = attend(k_ref[t_lat:, :], v_ref[t_lat:, :])


def _attention(q, k, v, t_lat):
    b, nh, s, _ = q.shape
    tq = TOK_TILE
    kern = functools.partial(_attn_kernel, n_lat_tiles=t_lat // tq, t_lat=t_lat)
    return pl.pallas_call(
        kern,
        grid=(b, nh, s // tq),
        in_specs=[pl.BlockSpec((None, None, tq, q.shape[-1]), lambda i, h, j: (i, h, j, 0)),
                  pl.BlockSpec((None, None, s, k.shape[-1]), lambda i, h, j: (i, h, 0, 0)),
                  pl.BlockSpec((None, None, s, v.shape[-1]), lambda i, h, j: (i, h, 0, 0))],
        out_specs=pl.BlockSpec((None, tq, MLA_V), lambda i, h, j: (i, j, h)),
        out_shape=jax.ShapeDtypeStruct((b, s, nh * MLA_V), BF16),
        compiler_params=_cparams("parallel", "parallel", "arbitrary"),
        name="mla_attention",
    )(q, k, v)


def _post_mixer_kernel(x_ref, y_ref, wo_ref, mod_ref, g2_ref, rt_ref, x1_ref, bm_ref, lg_ref):
    mod = mod_ref[...]
    x1 = x_ref[...] + mod[2:3] * jnp.dot(y_ref[...], wo_ref[...], preferred_element_type=F32)
    x1_ref[...] = x1
    bm = _norm_mod(x1, g2_ref[...], mod[4:5], mod[3:4]).astype(BF16)
    bm_ref[...] = bm
    lg_ref[...] = lax.dot_general(rt_ref[...], bm, _NT, preferred_element_type=F32)


def _post_mixer(x, y, wo, mod, g2, router_t, n_lat_tiles):
    b, s, d = x.shape
    tm = TOK_TILE
    const = lambda shape: pl.BlockSpec(shape, lambda i, j: (0,) * len(shape))
    return pl.pallas_call(
        _post_mixer_kernel,
        grid=(b, s // tm),
        in_specs=[pl.BlockSpec((None, tm, d), lambda i, j: (i, j, 0)),
                  pl.BlockSpec((None, tm, y.shape[-1]), lambda i, j: (i, j, 0)),
                  const(wo.shape),
                  pl.BlockSpec((None, None, 6, d), lambda i, j: (i, (j >= n_lat_tiles).astype(I32), 0, 0)),
                  const(g2.shape), const(router_t.shape)],
        out_specs=[pl.BlockSpec((None, tm, d), lambda i, j: (i, j, 0)),
                   pl.BlockSpec((None, tm, d), lambda i, j: (i, j, 0)),
                   pl.BlockSpec((None, N_EXPERTS, tm), lambda i, j: (i, 0, j))],
        out_shape=[jax.ShapeDtypeStruct((b, s, d), F32),
                   jax.ShapeDtypeStruct((b, s, d), BF16),
                   jax.ShapeDtypeStruct((b, N_EXPERTS, s), F32)],
        compiler_params=_cparams("parallel", "parallel"),
        name="post_mixer",
    )(x, y, wo, mod, g2, router_t)


def _select_kernel(lg_ref, pos_ref, aff_ref, off_ref, *, cap, tt):
    lg = lg_ref[...]
    ne, t = lg.shape
    m = jnp.max(lg, axis=0, keepdims=True)
    ex = jnp.exp(lg - m)
    aff = ex / jnp.sum(ex, axis=0, keepdims=True)
    aff_ref[...] = aff
    bits = pltpu.bitcast(aff, I32)

    def search(i, prefix):
        cand = prefix | jnp.left_shift(jnp.int32(1), 30 - i)
        cnt = jnp.sum(jnp.where(bits >= cand, 1.0, 0.0), axis=1, keepdims=True)
        return jnp.where(cnt >= cap, cand, prefix)

    thr = lax.fori_loop(0, 31, search, jnp.zeros((ne, 1), I32))
    gt = bits > thr
    eq = bits == thr
    need = cap - jnp.sum(jnp.where(gt, 1.0, 0.0), axis=1, keepdims=True)

    nch = t // LANE
    tri = (lax.broadcasted_iota(I32, (LANE, LANE), 0) <= lax.broadcasted_iota(I32, (LANE, LANE), 1))
    tri = jnp.where(tri, 1.0, 0.0).astype(BF16)

    def chunk(a, c):
        return a[:, c * LANE:(c + 1) * LANE]

    off = jnp.zeros((ne, 1), F32)
    sel = []
    for c in range(nch):
        eqc = jnp.where(chunk(eq, c), 1.0, 0.0)
        inc = jnp.dot(eqc.astype(BF16), tri, preferred_element_type=F32)
        rank = inc - eqc + off
        off = off + inc[:, LANE - 1:LANE]
        sel.append(jnp.where(chunk(gt, c) | (chunk(eq, c) & (rank < need)), 1.0, 0.0))

    lane = lax.broadcasted_iota(I32, (ne, LANE), 1)
    offs = jnp.zeros((ne, LANE), F32)
    off = jnp.zeros((ne, 1), F32)
    per_tile = tt // LANE
    for c in range(nch):
        if c % per_tile == 0:
            offs = jnp.where(lane == c // per_tile, off, offs)
        inc = jnp.dot(sel[c].astype(BF16), tri, preferred_element_type=F32)
        pos = inc - sel[c] + off
        off = off + inc[:, LANE - 1:LANE]
        pos_ref[:, c * LANE:(c + 1) * LANE] = jnp.where(sel[c] > 0.0, pos, -1.0).astype(I32)
    offs = jnp.where(lane == nch // per_tile, off, offs)
    off_ref[...] = offs.astype(I32)


def _select(logits_t, t0_blk, t, cap):
    b, ne, _ = logits_t.shape
    tt = min(TOK_TILE, t)
    kern = functools.partial(_select_kernel, cap=cap, tt=tt)
    return pl.pallas_call(
        kern,
        grid=(b,),
        in_specs=[pl.BlockSpec((None, ne, t), lambda i: (i, 0, t0_blk))],
        out_specs=[pl.BlockSpec((None, ne, t), lambda i: (i, 0, 0)),
                   pl.BlockSpec((None, ne, t), lambda i: (i, 0, 0)),
                   pl.BlockSpec((None, ne, LANE), lambda i: (i, 0, 0))],
        out_shape=[jax.ShapeDtypeStruct((b, ne, t), I32),
                   jax.ShapeDtypeStruct((b, ne, t), F32),
                   jax.ShapeDtypeStruct((b, ne, LANE), I32)],
        compiler_params=_cparams("parallel"),
        name="moe_select",
    )(logits_t)


def _window_onehot(posr, lo, cap, w, rows):
    base = pl.multiple_of(jnp.minimum((lo // SUBLANE) * SUBLANE, cap - w), SUBLANE)
    hit = ((posr - base) == rows) & (posr >= lo)
    return base, hit


def _gather_kernel(off_sm, h_ref, pos_ref, aff_ref, xe_ref, gs_ref, acc, gacc, *, cap, tt, nt, w):
    bi, e = pl.program_id(0), pl.program_id(1)
    ne = pl.num_programs(1)
    acc[...] = jnp.zeros_like(acc)
    gacc[...] = jnp.zeros_like(gacc)
    rows = lax.broadcasted_iota(I32, (w, tt), 0)
    obase = (bi * ne + e) * (nt + 1)

    def tile(j, carry):
        p0, p1 = off_sm[obase + j], off_sm[obase + j + 1]
        posr = pos_ref[pl.ds(j, 1), :]
        affr = aff_ref[pl.ds(j, 1), :]
        ht = h_ref[pl.ds(pl.multiple_of(j * tt, tt), tt), :]

        def window(lo):
            base, hit = _window_onehot(posr, lo, cap, w, rows)
            onehot = jnp.where(hit, 1.0, 0.0).astype(BF16)
            acc[pl.ds(base, w), :] += jnp.dot(onehot, ht, preferred_element_type=F32)
            gacc[pl.ds(base, w), :] += jnp.sum(jnp.where(hit, affr, 0.0), axis=1, keepdims=True)
            return base + w

        nxt = window(p0)

        @pl.when(p1 > nxt)
        def _():
            window(nxt)

        return carry

    lax.fori_loop(0, nt, tile, 0)
    xe_ref[...] = acc[...].astype(BF16)
    gs_ref[...] = gacc[...]


def _gather(off_flat, bm, pos4, aff4, t0_blk, t, cap):
    b, _, d = bm.shape
    ne, nt, tt = pos4.shape[1], pos4.shape[2], pos4.shape[3]
    w = min(tt, cap)
    kern = functools.partial(_gather_kernel, cap=cap, tt=tt, nt=nt, w=w)
    grid_spec = pltpu.PrefetchScalarGridSpec(
        num_scalar_prefetch=1,
        grid=(b, ne),
        in_specs=[pl.BlockSpec((None, t, d), lambda i, e, o: (i, t0_blk, 0)),
                  pl.BlockSpec((None, None, nt, tt), lambda i, e, o: (i, e, 0, 0)),
                  pl.BlockSpec((None, None, nt, tt), lambda i, e, o: (i, e, 0, 0))],
        out_specs=[pl.BlockSpec((None, None, cap, d), lambda i, e, o: (e, i, 0, 0)),
                   pl.BlockSpec((None, None, cap, 1), lambda i, e, o: (e, i, 0, 0))],
        scratch_shapes=[pltpu.VMEM((cap, d), F32), pltpu.VMEM((cap, 1), F32)])
    return pl.pallas_call(
        kern,
        grid_spec=grid_spec,
        out_shape=[jax.ShapeDtypeStruct((ne, b, cap, d), BF16),
                   jax.ShapeDtypeStruct((ne, b, cap, 1), F32)],
        compiler_params=_cparams("parallel", "arbitrary"),
        name="moe_gather",
    )(off_flat, bm, pos4, aff4)


def _ffn_kernel(x_ref, gs_ref, w1_ref, w3_ref, w2_ref, o_ref):
    x = x_ref[...]
    h1 = jnp.dot(x, w1_ref[...], preferred_element_type=F32)
    h3 = jnp.dot(x, w3_ref[...], preferred_element_type=F32)
    hid = (h1 * _sigmoid(h1)) * h3
    o_ref[...] = jnp.dot(hid.astype(BF16), w2_ref[...], preferred_element_type=F32) * gs_ref[...]


def _ffn(xe, gs, w1, w3, w2):
    ne, r, d = xe.shape
    f = w1.shape[-1]
    tr = min(r, 512)
    return pl.pallas_call(
        _ffn_kernel,
        grid=(ne, r // tr),
        in_specs=[pl.BlockSpec((None, tr, d), lambda e, i: (e, i, 0)),
                  pl.BlockSpec((None, tr, 1), lambda e, i: (e, i, 0)),
                  pl.BlockSpec((None, d, f), lambda e, i: (e, 0, 0)),
                  pl.BlockSpec((None, d, f), lambda e, i: (e, 0, 0)),
                  pl.BlockSpec((None, f, d), lambda e, i: (e, 0, 0))],
        out_specs=pl.BlockSpec((None, tr, d), lambda e, i: (e, i, 0)),
        out_shape=jax.ShapeDtypeStruct((ne, r, d), F32),
        compiler_params=_cparams("parallel", "arbitrary"),
        name="moe_ffn",
    )(xe, gs, w1, w3, w2)


def _combine_kernel(off_sm, x1_ref, mod_ref, pos_ref, ye_ref, o_ref, *, cap, tt, nt, w, tiles_per_blk):
    bi, ci, e = pl.program_id(0), pl.program_id(1), pl.program_id(2)
    ne = pl.num_programs(2)

    @pl.when(e == 0)
    def _():
        o_ref[...] = jnp.zeros_like(o_ref)

    rows = lax.broadcasted_iota(I32, (w, tt), 0)
    obase = (bi * ne + e) * (nt + 1)

    def tile(jj, carry):
        j = ci * tiles_per_blk + jj
        p0, p1 = off_sm[obase + j], off_sm[obase + j + 1]
        posr = pos_ref[pl.ds(j, 1), :]
        r0 = pl.multiple_of(jj * tt, tt)

        def window(lo):
            base, hit = _window_onehot(posr, lo, cap, w, rows)
            onehot = jnp.where(hit, 1.0, 0.0).astype(BF16)
            hi, lo_part = _split2(ye_ref[pl.ds(base, w), :])
            d = functools.partial(lax.dot_general, dimension_numbers=_TN, preferred_element_type=F32)
            o_ref[pl.ds(r0, tt), :] += d(onehot, hi) + d(onehot, lo_part)
            return base + w

        nxt = window(p0)

        @pl.when(p1 > nxt)
        def _():
            window(nxt)

        return carry

    lax.fori_loop(0, tiles_per_blk, tile, 0)

    @pl.when(e == ne - 1)
    def _():
        o_ref[...] = x1_ref[...] + mod_ref[...][5:6] * o_ref[...]


def _combine(off_flat, x1, mod, stream, pos4, ye, t0_blk, t, cap):
    b, s, d = x1.shape
    ne, nt, tt = pos4.shape[1], pos4.shape[2], pos4.shape[3]
    w = min(tt, cap)
    tc = min(t, 2048)
    nblk = t // tc
    kern = functools.partial(_combine_kernel, cap=cap, tt=tt, nt=nt, w=w, tiles_per_blk=tc // tt)
    grid_spec = pltpu.PrefetchScalarGridSpec(
        num_scalar_prefetch=1,
        grid=(b, nblk, ne),
        in_specs=[pl.BlockSpec((None, tc, d), lambda i, c, e, o: (i, t0_blk * nblk + c, 0)),
                  pl.BlockSpec((None, None, 6, d), lambda i, c, e, o: (i, stream, 0, 0)),
                  pl.BlockSpec((None, None, nt, tt), lambda i, c, e, o: (i, e, 0, 0)),
                  pl.BlockSpec((None, None, cap, d), lambda i, c, e, o: (e, i, 0, 0))],
        out_specs=pl.BlockSpec((None, tc, d), lambda i, c, e, o: (i, t0_blk * nblk + c, 0)))
    return pl.pallas_call(
        kern,
        grid_spec=grid_spec,
        out_shape=jax.ShapeDtypeStruct((b, s, d), F32),
        input_output_aliases={1: 0},
        compiler_params=_cparams("parallel", "parallel", "arbitrary"),
        name="moe_combine",
    )(off_flat, x1, mod, pos4, ye)


def _moe_stream(x1, bm, logits_t, mod, stream, t0_blk, t, w1, w3, w2):
    b = x1.shape[0]
    cap = CAPACITY_FACTOR * t // N_EXPERTS
    pos, aff, offs = _select(logits_t, t0_blk, t, cap)
    tt = min(TOK_TILE, t)
    nt = t // tt
    pos4 = pos.reshape(b, N_EXPERTS, nt, tt)
    aff4 = aff.reshape(b, N_EXPERTS, nt, tt)
    off_flat = offs[:, :, :nt + 1].reshape(-1)
    xe, gs = _gather(off_flat, bm, pos4, aff4, t0_blk, t, cap)
    d = x1.shape[-1]
    ye = _ffn(xe.reshape(N_EXPERTS, b * cap, d), gs.reshape(N_EXPERTS, b * cap, 1), w1, w3, w2)
    return _combine(off_flat, x1, mod, stream, pos4, ye.reshape(N_EXPERTS, b, cap, d), t0_blk, t, cap)


def _rwkv_stream_kernel(*refs, t_lat, has_vres):
    if has_vres:
        (x_ref, xp_ref, xn_ref, mod_ref, g1_ref, mu_ref, wr_ref, wk_ref, wv_ref, w1_ref, w2_ref, w0_ref,
         a1_ref, a2_ref, a0_ref, gg1_ref, gg2_ref, kk_ref, ka_ref, rk_ref, seg_ref, segt_ref,
         vf_ref, v0_ref, v1_ref, v2_ref,
         r_out, v_out, kk_out, lw_out, kd_out, b_out, g_out, bonus_out) = refs
    else:
        (x_ref, xp_ref, xn_ref, mod_ref, g1_ref, mu_ref, wr_ref, wk_ref, wv_ref, w1_ref, w2_ref, w0_ref,
         a1_ref, a2_ref, a0_ref, gg1_ref, gg2_ref, kk_ref, ka_ref, rk_ref, seg_ref, segt_ref,
         r_out, v_out, kk_out, lw_out, kd_out, b_out, g_out, bonus_out) = refs
    j = pl.program_id(1)
    mod = mod_ref[...]
    g1 = g1_ref[...]
    nm = lambda x: _norm_mod(x, g1, mod[1:2], mod[0:1])
    h = nm(x_ref[...])
    tm, d = h.shape
    hp = nm(xp_ref[...])[SUBLANE - 1:SUBLANE]
    hn = nm(xn_ref[...])[0:1]
    row = lax.broadcasted_iota(I32, (tm, 1), 0)
    grow = row + j * tm
    s_tot = pl.num_programs(1) * tm
    first = (grow == 0) | (grow == t_lat)
    last = (grow == t_lat - 1) | (grow == s_tot - 1)
    prev = jnp.where(row == 0, hp, pltpu.roll(h, 1, 0))
    prev = jnp.where(first, 0.0, prev)
    nxt = jnp.where(row == tm - 1, hn, pltpu.roll(h, tm - 1, 0))
    nxt = jnp.where(last, 0.0, nxt)
    xx = 0.5 * (prev + nxt) - h
    mu = mu_ref[...]
    xs = [h + xx * mu[m:m + 1] for m in range(6)]
    r = _bdot(xs[0], wr_ref[...])
    k = _bdot(xs[1], wk_ref[...])
    v = _bdot(xs[2], wv_ref[...])
    if has_vres:
        gate = _sigmoid(v0_ref[...] + _bdot(_bdot(xs[2], v1_ref[...]), v2_ref[...]))
        v = v + (vf_ref[...] - v) * gate
    wz = w0_ref[...] + _bdot(jnp.tanh(_bdot(xs[3], w1_ref[...])), w2_ref[...])
    lw = -math.exp(-0.5) * _sigmoid(wz)
    a = _sigmoid(a0_ref[...] + _bdot(_bdot(xs[4], a1_ref[...]), a2_ref[...]))
    g = _bdot(_sigmoid(_bdot(xs[5], gg1_ref[...])), gg2_ref[...])
    seg, segt = seg_ref[...], segt_ref[...]
    segsum = lambda t: _dot_exact_rhs01(_dot_exact_rhs01(t, seg), segt)
    kkr = k * kk_ref[...]
    kk = kkr / jnp.maximum(jnp.sqrt(segsum(kkr * kkr)), 1e-12)
    ka = ka_ref[...]
    kd0 = k * (1.0 + (a[:, :d] - 1.0) * ka)
    kd1 = k * (1.0 + (a[:, d:] - 1.0) * ka)
    r_out[...] = r
    v_out[...] = v
    kk_out[...] = kk
    lw_out[...] = lw
    kd_out[:, :d] = kd0
    kd_out[:, d:] = kd1
    b_out[:, :d] = kk * a[:, :d]
    b_out[:, d:] = kk * a[:, d:]
    g_out[...] = g
    bonus_out[...] = segsum(r * (kd0 + kd1) * rk_ref[...]) * v


def _rwkv_stream(x, mod, g1, wts, vres, t_lat):
    b, s, d = x.shape
    tm = TOK_TILE
    nlt = t_lat // tm
    nsub = tm // SUBLANE
    const = lambda a: pl.BlockSpec(a.shape, lambda i, j: (0,) * a.ndim)
    tok = lambda w: pl.BlockSpec((None, tm, w), lambda i, j: (i, j, 0))
    in_specs = [tok(d),
                pl.BlockSpec((None, SUBLANE, d), lambda i, j: (i, jnp.maximum(j * nsub - 1, 0), 0)),
                pl.BlockSpec((None, SUBLANE, d), lambda i, j: (i, jnp.minimum((j + 1) * nsub, s // SUBLANE - 1), 0)),
                pl.BlockSpec((None, None, 6, d), lambda i, j: (i, (j >= nlt).astype(I32), 0, 0)),
                const(g1)] + [const(a) for a in wts]
    args = [x, x, x, mod, g1] + list(wts)
    if vres is not None:
        vf, v0, v1, v2 = vres
        in_specs += [tok(d), const(v0), const(v1), const(v2)]
        args += [vf, v0, v1, v2]
    kern = functools.partial(_rwkv_stream_kernel, t_lat=t_lat, has_vres=vres is not None)
    widths = [d, d, d, 2 * d, 2 * d, 2 * d, d, d]
    return pl.pallas_call(
        kern,
        grid=(b, s // tm),
        in_specs=in_specs,
        out_specs=[tok(w) for w in widths],
        out_shape=[jax.ShapeDtypeStruct((b, s, w), F32) for w in widths],
        compiler_params=_cparams("parallel", "parallel"),
        name="rwkv_stream",
    )(*args)


def _scan_kernel(r_ref, v_ref, kk_ref, lw_ref, kd_ref, b_ref, y_ref, h_ref, *, reverse, nchunk):
    L = SCAN_CHUNK

    @pl.when(pl.program_id(2) == 0)
    def _():
        h_ref[...] = jnp.zeros_like(h_ref)

    n2 = 2 * L
    rr = lax.broadcasted_iota(I32, (n2, n2), 0)
    cc = lax.broadcasted_iota(I32, (n2, n2), 1)
    tr, tc = rr % L, cc % L
    same = (rr // L) == (cc // L)
    before = (tr < tc) if reverse else (tr > tc)
    strict = same & before
    incl = same & (before | (tr == tc))
    eye = rr == cc
    blk16 = (rr // 16) == (cc // 16)
    blk32 = (rr // 32) == (cc // 32)
    ri = lax.broadcasted_iota(I32, (L, L), 0)
    ci = lax.broadcasted_iota(I32, (L, L), 1)
    tri = jnp.where((ci >= ri) if reverse else (ci <= ri), 1.0, 0.0).astype(BF16)
    head0 = lax.broadcasted_iota(I32, (L, LANE), 1) < RW_HEAD

    def stack(x):
        return jnp.concatenate([jnp.where(head0, x, 0.0), jnp.where(head0, 0.0, x)], axis=0)

    order = list(range(nchunk - 1, -1, -1) if reverse else range(nchunk))
    npp = y_ref.shape[-1] // LANE
    units = [(pp, ch) for pp in range(npp) for ch in order]

    def each(f, *lists):
        return [f(*a) for a in zip(*lists)]

    def ld(ref):
        return [ref[pl.ds(ch * L, L), pl.ds(pp * LANE, LANE)] for pp, ch in units]

    dd = functools.partial(jnp.dot, preferred_element_type=F32)
    lw = ld(lw_ref)
    lws = each(_split3, lw)
    cs = each(lambda t: dd(tri, t[0]) + (dd(tri, t[1]) + dd(tri, t[2])), lws)
    total = each(lambda c: c[0:1] if reverse else c[L - 1:L], cs)
    gam = each(jnp.exp, cs)
    ginv = each(lambda c: jnp.exp(-c), cs)
    gprev = each(lambda c, l: jnp.exp(c - l), cs, lw)
    gend = each(lambda t, c: jnp.exp(t - c), total, cs)
    kk, bb, kd, r, v = ld(kk_ref), ld(b_ref), ld(kd_ref), ld(r_ref), ld(v_ref)
    mul_stack = lambda a, g: stack(a * g)
    rtm = each(mul_stack, r, gam)
    kkm, btm, ktm, rts = (each(_split2, each(mul_stack, a, g))
                          for a, g in ((kk, gprev), (bb, ginv), (kd, ginv), (r, gam)))
    bhm, khm = (each(_split2, each(mul_stack, a, gend)) for a in (bb, kd))
    vm = each(_split2, each(stack, v))
    n = each(lambda a, c: jnp.where(strict, _mm3(a, c, _NT), 0.0), kkm, btm)
    m = each(lambda a, c: jnp.where(strict, _mm3(a, c, _NT), 0.0), kkm, ktm)
    ab = each(lambda a, c: jnp.where(incl, _mm3(a, c, _NT), 0.0), rts, btm)
    ak = each(lambda a, c: jnp.where(incl, _mm3(a, c, _NT), 0.0), rts, ktm)
    nd = each(lambda t: jnp.where(blk16, t, 0.0), n)
    x = each(lambda t: jnp.where(eye, 1.0, 0.0) - t, nd)
    pws = each(_split2, nd)
    for _ in range(3):
        pws = each(lambda t: _split2(_mm3(t, t)), pws)
        x = each(lambda a, c: a + _mm3(_split2(a), c), x, pws)
    for inner, outer in ((blk16, blk32), (blk32, same)):
        c = each(lambda t: _split2(jnp.where(outer & jnp.logical_not(inner), t, 0.0)), n)
        xs = each(_split2, x)
        xc = each(lambda a, t: _split2(_mm3(a, t)), xs, c)
        x = each(lambda a, t, u: a - _mm3(t, u), x, xc, xs)
    xs = each(_split2, x)
    mv = each(lambda a, c: _split2(_mm3(_split2(a), c)), m, vm)
    w = each(lambda a, c: _split2(_mm3(a, c)), xs, kkm)
    u0 = each(lambda a, c: _split2(_mm3(a, c)), xs, mv)
    p = each(lambda t, a, c: jnp.where(eye, jnp.exp(t), 0.0) - _mm3(a, c, _TN), total, bhm, w)
    q = each(lambda a, c, e, f: _mm3(a, c, _TN) - _mm3(e, f, _TN), khm, vm, bhm, u0)
    abs_, aks = each(_split2, ab), each(_split2, ak)
    rres = each(lambda a, c, e: a - _mm3(c, e), rtm, abs_, w)
    y0 = each(lambda a, c, e, f: _mm3(a, c) - _mm3(e, f), aks, vm, abs_, u0)
    for i, (pp, ch) in enumerate(units):
        h0 = _split2(h_ref[pp])
        y = _mm3(_split2(rres[i]), h0) + y0[i]
        y_ref[pl.ds(ch * L, L), pl.ds(pp * LANE, LANE)] = y[:L] + y[L:]
        h_ref[pp] = _mm3(_split2(p[i]), h0) + q[i]


def _scan(r, v, kk, lw, kd, bb, t_lat, reverse):
    b, s, d = r.shape
    tb = TOK_TILE
    nl, nc = t_lat // tb, (s - t_lat) // tb
    z = 1 if reverse else 0

    def blk(c):
        if reverse:
            return jnp.where(c < nc, nl + nc - 1 - c, nl - 1 - (c - nc))
        return jnp.where(c < nc, nl + c, c - nc)

    wl = SCAN_PAIRS * LANE
    ngrp = d // wl
    shared = pl.BlockSpec((None, tb, wl), lambda i, p, c: (i, blk(c), p))
    dirn = pl.BlockSpec((None, tb, wl), lambda i, p, c: (i, blk(c), z * ngrp + p))
    kern = functools.partial(_scan_kernel, reverse=reverse, nchunk=tb // SCAN_CHUNK)
    return pl.pallas_call(
        kern,
        grid=(b, ngrp, nl + nc),
        in_specs=[shared, shared, shared, dirn, dirn, dirn],
        out_specs=pl.BlockSpec((None, tb, wl), lambda i, p, c: (i, blk(c), p)),
        out_shape=jax.ShapeDtypeStruct((b, s, d), F32),
        scratch_shapes=[pltpu.VMEM((SCAN_PAIRS, LANE, LANE), F32)],
        compiler_params=_cparams("parallel", "parallel", "arbitrary"),
        name="wkv_scan_bwd" if reverse else "wkv_scan_fwd",
    )(r, v, kk, lw, kd, bb)


def _rwkv_out_kernel(yf_ref, yb_ref, bonus_ref, g_ref, lg_ref, lb_ref, seg_ref, segt_ref, o_ref):
    seg, segt = seg_ref[...], segt_ref[...]
    segmean = lambda t: _dot_exact_rhs01(_dot_exact_rhs01(t, seg), segt) * (1.0 / RW_HEAD)
    y = yf_ref[...] + yb_ref[...]
    dlt = y - segmean(y)
    var = segmean(dlt * dlt)
    yn = (dlt * lax.rsqrt(var + RW_LNX_EPS)) * lg_ref[...] + lb_ref[...]
    o_ref[...] = ((yn + bonus_ref[...]) * g_ref[...]).astype(BF16)


def _rwkv_out(yf, yb, bonus, g, lnx_g, lnx_b, seg, segt):
    b, s, d = yf.shape
    tm = TOK_TILE
    tok = pl.BlockSpec((None, tm, d), lambda i, j: (i, j, 0))
    const = lambda a: pl.BlockSpec(a.shape, lambda i, j: (0,) * a.ndim)
    return pl.pallas_call(
        _rwkv_out_kernel,
        grid=(b, s // tm),
        in_specs=[tok, tok, tok, tok, const(lnx_g), const(lnx_b), const(seg), const(segt)],
        out_specs=tok,
        out_shape=jax.ShapeDtypeStruct((b, s, d), BF16),
        compiler_params=_cparams("parallel", "parallel"),
        name="rwkv_out",
    )(yf, yb, bonus, g, lnx_g, lnx_b, seg, segt)


def _final_kernel(x_ref, g_ref, o_ref):
    o_ref[...] = _rms(x_ref[...], g_ref[...])


def _final_norm(x, g, t_lat):
    b, _, d = x.shape
    tm = TOK_TILE
    return pl.pallas_call(
        _final_kernel,
        grid=(b, t_lat // tm),
        in_specs=[pl.BlockSpec((None, tm, d), lambda i, j: (i, j, 0)),
                  pl.BlockSpec(g.shape, lambda i, j: (0, 0))],
        out_specs=pl.BlockSpec((None, tm, d), lambda i, j: (i, j, 0)),
        out_shape=jax.ShapeDtypeStruct((b, t_lat, d), F32),
        compiler_params=_cparams("parallel", "parallel"),
        name="final_norm",
    )(x, g)


def _rope_tables(t_lat, t_ctx):
    rows = t_lat // GRID_W
    r = jnp.broadcast_to(jnp.arange(rows)[:, None], (rows, GRID_W)).reshape(-1).astype(F32)
    col = jnp.broadcast_to(jnp.arange(GRID_W)[None, :], (rows, GRID_W)).reshape(-1).astype(F32)
    n_freq = MLA_ROPE // 4
    inv = ROPE_THETA ** (-jnp.arange(n_freq, dtype=F32) / n_freq)
    ang = jnp.concatenate([r[:, None] * inv, col[:, None] * inv], axis=-1)
    cos, sin = jnp.cos(ang), jnp.sin(ang)
    cosf = jnp.concatenate([cos, cos, cos, cos], axis=-1)
    sins = jnp.concatenate([-sin, sin, -sin, sin], axis=-1)
    cosf = jnp.concatenate([cosf, jnp.ones((t_ctx, LANE), F32)], axis=0)
    sins = jnp.concatenate([sins, jnp.zeros((t_ctx, LANE), F32)], axis=0)
    return cosf, sins


def _blockdiag2(w):
    z = jnp.zeros_like(w[0])
    return jnp.concatenate([jnp.concatenate([w[0], z], axis=1), jnp.concatenate([z, w[1]], axis=1)], axis=0)


def kernel(x, c, ctx, c_ctx, mod_w, mod_b, norm1_g, norm2_g, final_g, mla_w_in, mla_g_q, mla_g_kv, mla_w_uq, mla_w_ukv, mla_w_o, rw_mu, rw_w_rkv, rw_w0, rw_w1, rw_w2, rw_a0, rw_a1, rw_a2, rw_v0, rw_v1, rw_v2, rw_g1, rw_g2, rw_k_k, rw_k_a, rw_r_k, rw_lnx_g, rw_lnx_b, rw_w_o, moe_router, moe_w1, moe_w3, moe_w2):
    b, t_lat, d = x.shape
    t_ctx = ctx.shape[1]
    depth = mod_w.shape[0]
    assert b + 1 <= SUBLANE and t_lat % TOK_TILE == 0 and t_ctx % TOK_TILE == 0 and d % LANE == 0
    n_lat_tiles = t_lat // TOK_TILE
    ctx_blk = t_lat // t_ctx
    assert ctx_blk * t_ctx == t_lat

    cc = jnp.concatenate([c, c_ctx[None], jnp.zeros((SUBLANE - b - 1, d), F32)], axis=0)
    mods = _modulation(cc, mod_w, mod_b).reshape(depth, SUBLANE, 6, d)
    mod_all = jnp.stack([mods[:, :b], jnp.broadcast_to(mods[:, b:b + 1], (depth, b, 6, d))], axis=2)

    xs = jnp.concatenate([x, ctx], axis=1)
    cosf, sins = _rope_tables(t_lat, t_ctx)
    row = lambda a: a.reshape(1, -1)
    seg = (jnp.arange(d)[:, None] // RW_HEAD == jnp.arange(LANE)[None, :]).astype(BF16)
    segt = seg.T

    v_first = None
    for layer in range(depth):
        need_ctx = layer < depth - 1
        mod = mod_all[layer]
        j = layer // 2
        if layer % 2 == 0:
            win = jnp.pad(mla_w_in[j], ((0, 0), (0, 4 * LANE - mla_w_in.shape[-1]))).astype(BF16)
            wuq = mla_w_uq[j].reshape(MLA_Q_LORA, MLA_HEADS, MLA_NOPE + MLA_ROPE)
            wuq = jnp.concatenate([wuq[..., :MLA_NOPE].reshape(MLA_Q_LORA, -1),
                                   wuq[..., MLA_NOPE:].reshape(MLA_Q_LORA, -1)], axis=1).astype(BF16)
            q, k, v = _mla_proj(xs, mod, row(norm1_g[layer]), win, row(mla_g_q[j]), row(mla_g_kv[j]),
                                wuq, mla_w_ukv[j].astype(BF16), cosf, sins, n_lat_tiles)
            y = _attention(q, k, v, t_lat)
            wo = mla_w_o[j].astype(BF16)
        else:
            lora = lambda w: jnp.concatenate([w[0], w[1]], axis=1).astype(BF16)
            wts = [rw_mu[j], rw_w_rkv[j, 0].astype(BF16), rw_w_rkv[j, 1].astype(BF16), rw_w_rkv[j, 2].astype(BF16),
                   lora(rw_w1[j]), _blockdiag2(rw_w2[j]).astype(BF16), rw_w0[j].reshape(1, -1),
                   lora(rw_a1[j]), _blockdiag2(rw_a2[j]).astype(BF16), rw_a0[j].reshape(1, -1),
                   rw_g1[j].astype(BF16), rw_g2[j].astype(BF16),
                   row(rw_k_k[j]), row(rw_k_a[j]), row(rw_r_k[j]), seg, segt]
            vres = None
            if j > 0:
                pad = LANE - rw_v1.shape[-1]
                vres = (v_first, row(rw_v0[j - 1]), jnp.pad(rw_v1[j - 1], ((0, 0), (0, pad))).astype(BF16),
                        jnp.pad(rw_v2[j - 1], ((0, pad), (0, 0))).astype(BF16))
            r, v, kk, lw, kd, bb, g, bonus = _rwkv_stream(xs, mod, row(norm1_g[layer]), wts, vres, t_lat)
            if j == 0:
                v_first = v
            yf = _scan(r, v, kk, lw, kd, bb, t_lat, reverse=False)
            yb = _scan(r, v, kk, lw, kd, bb, t_lat, reverse=True)
            y = _rwkv_out(yf, yb, bonus, g, row(rw_lnx_g[j]), row(rw_lnx_b[j]), seg, segt)
            wo = rw_w_o[j].astype(BF16)
        x1, bm, logits_t = _post_mixer(xs, y, wo, mod, row(norm2_g[layer]), moe_router[layer].T.astype(BF16),
                                       n_lat_tiles)
        w1, w3, w2 = moe_w1[layer].astype(BF16), moe_w3[layer].astype(BF16), moe_w2[layer].astype(BF16)
        xs = _moe_stream(x1, bm, logits_t, mod, 0, 0, t_lat, w1, w3, w2)
        if need_ctx:
            xs = _moe_stream(xs, bm, logits_t, mod, 1, ctx_blk, t_ctx, w1, w3, w2)
    return _final_norm(xs, row(final_g), t_lat)
```

```python
import functools
import math

import jax
import jax.numpy as jnp
from jax import lax
from jax.experimental import pallas as pl
from jax.experimental.pallas import tpu as pltpu

F32, BF16, I32 = jnp.float32, jnp.bfloat16, jnp.int32

GRID_W = 64
MLA_HEADS, MLA_NOPE, MLA_ROPE, MLA_V = 8, 128, 64, 128
MLA_Q_LORA, MLA_KV_LORA = 256, 128
MLA_SCALE = (MLA_NOPE + MLA_ROPE) ** -0.5
ROPE_THETA = 10000.0
RW_HEAD = 64
RW_LNX_EPS = 64e-5
N_EXPERTS = 16
CAPACITY_FACTOR = 2
RMS_EPS = 1e-6

LANE = 128
SUBLANE = 8
TOK_TILE = 256
SCAN_CHUNK = 64
SCAN_PAIRS = 4
VMEM_LIMIT = 56 * 1024 * 1024


def _cparams(*sem):
    return pltpu.CompilerParams(dimension_semantics=sem, vmem_limit_bytes=VMEM_LIMIT)


def _bdot(a, b):
    return jnp.dot(a.astype(BF16), b.astype(BF16), preferred_element_type=F32)


_NT = (((1,), (1,)), ((), ()))
_TN = (((0,), (0,)), ((), ()))
_NN = (((1,), (0,)), ((), ()))


def _split2(x):
    hi = x.astype(BF16)
    lo = (x - hi.astype(F32)).astype(BF16)
    return hi, lo


def _split3(x):
    hi = x.astype(BF16)
    r = x - hi.astype(F32)
    mid = r.astype(BF16)
    lo = (r - mid.astype(F32)).astype(BF16)
    return hi, mid, lo


def _mm3(a, b, dims=_NN):
    d = functools.partial(lax.dot_general, dimension_numbers=dims, preferred_element_type=F32)
    return d(a[0], b[0]) + (d(a[0], b[1]) + d(a[1], b[0]))


def _mm3_rows(lhs, b, dims=_NN):
    his = jnp.concatenate([a[0] for a in lhs], axis=0)
    los = jnp.concatenate([a[1] for a in lhs], axis=0)
    d = functools.partial(lax.dot_general, dimension_numbers=dims, preferred_element_type=F32)
    nr = his.shape[0]
    big = d(jnp.concatenate([his, los], axis=0), b[0])
    res = big[:nr] + (d(his, b[1]) + big[nr:])
    outs, o = [], 0
    for a in lhs:
        outs.append(res[o:o + a[0].shape[0]])
        o += a[0].shape[0]
    return outs


def _dot_exact_rhs01(x, m01):
    h, m, l = _split3(x)
    d = functools.partial(jnp.dot, preferred_element_type=F32)
    return d(h, m01) + (d(m, m01) + d(l, m01))


def _rms(x, g):
    return (x * lax.rsqrt(jnp.mean(x * x, axis=-1, keepdims=True) + RMS_EPS)) * g


def _norm_mod(x, g, scale, shift):
    return _rms(x, g) * (1.0 + scale) + shift


def _sigmoid(x):
    return 1.0 / (1.0 + jnp.exp(-x))


def _mod_kernel(c_ref, w_ref, b_ref, o_ref):
    s = c_ref[...]
    s = s * _sigmoid(s)
    o_ref[...] = _bdot(s, w_ref[...]) + b_ref[...]


def _modulation(cc, mod_w, mod_b):
    depth, d, n = mod_w.shape
    tn = n // 4
    return pl.pallas_call(
        _mod_kernel,
        grid=(depth, n // tn),
        in_specs=[pl.BlockSpec((SUBLANE, d), lambda l, j: (0, 0)),
                  pl.BlockSpec((None, d, tn), lambda l, j: (l, 0, j)),
                  pl.BlockSpec((None, 1, tn), lambda l, j: (l, 0, j))],
        out_specs=pl.BlockSpec((None, SUBLANE, tn), lambda l, j: (l, 0, j)),
        out_shape=jax.ShapeDtypeStruct((depth, SUBLANE, n), F32),
        compiler_params=_cparams("arbitrary", "arbitrary"),
        name="modulation",
    )(cc, mod_w, mod_b.reshape(depth, 1, n))


def _rope128(x, cosf, sins):
    lane = lax.broadcasted_iota(I32, x.shape, 1)
    first = (lane % MLA_ROPE) < (MLA_ROPE // 2)
    rot = jnp.where(first, pltpu.roll(x, LANE - MLA_ROPE // 2, 1), pltpu.roll(x, MLA_ROPE // 2, 1))
    return x * cosf + rot * sins


def _mla_proj_kernel(x_ref, mod_ref, g1_ref, win_ref, gq_ref, gkv_ref, wuq_ref, wukv_ref,
                     cos_ref, sin_ref, q_ref, k_ref, v_ref):
    mod = mod_ref[...]
    h = _norm_mod(x_ref[...], g1_ref[...], mod[1:2], mod[0:1])
    z = _bdot(h, win_ref[...])
    cq = _rms(z[:, :MLA_Q_LORA], gq_ref[...])
    ckv = _rms(z[:, MLA_Q_LORA:MLA_Q_LORA + MLA_KV_LORA], gkv_ref[...])
    kr = z[:, MLA_Q_LORA + MLA_KV_LORA:]
    q = _bdot(cq, wuq_ref[...])
    kv = _bdot(ckv, wukv_ref[...])
    cosf, sins = cos_ref[...], sin_ref[...]
    kr_lo = _rope128(kr, cosf, sins)
    kr_hi = pltpu.roll(kr_lo, MLA_ROPE, 1)
    nope_w = MLA_HEADS * MLA_NOPE
    for g in range(MLA_HEADS // 2):
        qr = _rope128(q[:, nope_w + LANE * g:nope_w + LANE * (g + 1)], cosf, sins)
        for hh in (2 * g, 2 * g + 1):
            qn = q[:, MLA_NOPE * hh:MLA_NOPE * (hh + 1)]
            q_ref[hh] = jnp.concatenate([qn, qr], axis=1).astype(BF16)
    for hh in range(MLA_HEADS):
        base = (MLA_NOPE + MLA_V) * hh
        krh = kr_lo if hh % 2 == 0 else kr_hi
        k_ref[hh] = jnp.concatenate([kv[:, base:base + MLA_NOPE], krh], axis=1).astype(BF16)
        v_ref[hh] = kv[:, base + MLA_NOPE:base + MLA_NOPE + MLA_V].astype(BF16)


def _mla_proj(x, mod, g1, win, gq, gkv, wuq, wukv, cosf, sins, n_lat_tiles):
    b, s, d = x.shape
    tm = TOK_TILE
    const = lambda shape: pl.BlockSpec(shape, lambda i, j: (0,) * len(shape))
    hspec = lambda w: pl.BlockSpec((None, MLA_HEADS, tm, w), lambda i, j: (i, 0, j, 0))
    return pl.pallas_call(
        _mla_proj_kernel,
        grid=(b, s // tm),
        in_specs=[pl.BlockSpec((None, tm, d), lambda i, j: (i, j, 0)),
                  pl.BlockSpec((None, None, 6, d), lambda i, j: (i, (j >= n_lat_tiles).astype(I32), 0, 0)),
                  const(g1.shape), const(win.shape), const(gq.shape), const(gkv.shape),
                  const(wuq.shape), const(wukv.shape),
                  pl.BlockSpec((tm, LANE), lambda i, j: (j, 0)),
                  pl.BlockSpec((tm, LANE), lambda i, j: (j, 0))],
        out_specs=[hspec(2 * LANE), hspec(2 * LANE), hspec(MLA_V)],
        out_shape=[jax.ShapeDtypeStruct((b, MLA_HEADS, s, 2 * LANE), BF16),
                   jax.ShapeDtypeStruct((b, MLA_HEADS, s, 2 * LANE), BF16),
                   jax.ShapeDtypeStruct((b, MLA_HEADS, s, MLA_V), BF16)],
        compiler_params=_cparams("parallel", "parallel"),
        name="mla_proj",
    )(x, mod, g1, win, gq, gkv, wuq, wukv, cosf, sins)


def _attn_kernel(q_ref, k_ref, v_ref, o_ref, *, n_lat_tiles, t_lat):
    j = pl.program_id(2)
    q = q_ref[...]

    def attend(k, v):
        s = lax.dot_general(q, k, _NT, preferred_element_type=F32)
        m = jnp.max(s, axis=-1, keepdims=True)
        p = jnp.exp2((s - m) * (MLA_SCALE * math.log2(math.e)))
        l = jnp.sum(p, axis=-1, keepdims=True)
        o = jnp.dot(p.astype(BF16), v, preferred_element_type=F32)
        return (o / l).astype(o_ref.dtype)

    @pl.when(j < n_lat_tiles)
    def _():
        o_ref[...] = attend(k_ref[...], v_ref[...])

    @pl.when(j >= n_lat_tiles)
    def _():
        o_ref[...] = attend(k_ref[t_lat:, :], v_ref[t_lat:, :])


def _attention(q, k, v, t_lat):
    b, nh, s, _ = q.shape
    tq = TOK_TILE
    kern = functools.partial(_attn_kernel, n_lat_tiles=t_lat // tq, t_lat=t_lat)
    return pl.pallas_call(
        kern,
        grid=(b, nh, s // tq),
        in_specs=[pl.BlockSpec((None, None, tq, q.shape[-1]), lambda i, h, j: (i, h, j, 0)),
                  pl.BlockSpec((None, None, s, k.shape[-1]), lambda i, h, j: (i, h, 0, 0)),
                  pl.BlockSpec((None, None, s, v.shape[-1]), lambda i, h, j: (i, h, 0, 0))],
        out_specs=pl.BlockSpec((None, tq, MLA_V), lambda i, h, j: (i, j, h)),
        out_shape=jax.ShapeDtypeStruct((b, s, nh * MLA_V), BF16),
        compiler_params=_cparams("parallel", "parallel", "arbitrary"),
        name="mla_attention",
    )(q, k, v)


def _post_mixer_kernel(x_ref, y_ref, wo_ref, mod_ref, g2_ref, rt_ref, x1_ref, bm_ref, lg_ref):
    mod = mod_ref[...]
    x1 = x_ref[...] + mod[2:3] * jnp.dot(y_ref[...], wo_ref[...], preferred_element_type=F32)
    x1_ref[...] = x1
    bm = _norm_mod(x1, g2_ref[...], mod[4:5], mod[3:4]).astype(BF16)
    bm_ref[...] = bm
    lg_ref[...] = lax.dot_general(rt_ref[...], bm, _NT, preferred_element_type=F32)


def _post_mixer(x, y, wo, mod, g2, router_t, n_lat_tiles):
    b, s, d = x.shape
    tm = TOK_TILE
    const = lambda shape: pl.BlockSpec(shape, lambda i, j: (0,) * len(shape))
    return pl.pallas_call(
        _post_mixer_kernel,
        grid=(b, s // tm),
        in_specs=[pl.BlockSpec((None, tm, d), lambda i, j: (i, j, 0)),
                  pl.BlockSpec((None, tm, y.shape[-1]), lambda i, j: (i, j, 0)),
                  const(wo.shape),
                  pl.BlockSpec((None, None, 6, d), lambda i, j: (i, (j >= n_lat_tiles).astype(I32), 0, 0)),
                  const(g2.shape), const(router_t.shape)],
        out_specs=[pl.BlockSpec((None, tm, d), lambda i, j: (i, j, 0)),
                   pl.BlockSpec((None, tm, d), lambda i, j: (i, j, 0)),
                   pl.BlockSpec((None, N_EXPERTS, tm), lambda i, j: (i, 0, j))],
        out_shape=[jax.ShapeDtypeStruct((b, s, d), F32),
                   jax.ShapeDtypeStruct((b, s, d), BF16),
                   jax.ShapeDtypeStruct((b, N_EXPERTS, s), F32)],
        compiler_params=_cparams("parallel", "parallel"),
        name="post_mixer",
    )(x, y, wo, mod, g2, router_t)


def _select_kernel(lg_ref, pos_ref, aff_ref, off_ref, *, cap, tt):
    lg = lg_ref[...]
    ne, t = lg.shape
    m = jnp.max(lg, axis=0, keepdims=True)
    ex = jnp.exp(lg - m)
    aff = ex / jnp.sum(ex, axis=0, keepdims=True)
    aff_ref[...] = aff
    bits = pltpu.bitcast(aff, I32)

    def search(i, prefix):
        cand = prefix | jnp.left_shift(jnp.int32(1), 30 - i)
        cnt = jnp.sum(jnp.where(bits >= cand, 1.0, 0.0), axis=1, keepdims=True)
        return jnp.where(cnt >= cap, cand, prefix)

    thr = lax.fori_loop(0, 31, search, jnp.zeros((ne, 1), I32))
    gt = bits > thr
    eq = bits == thr
    need = cap - jnp.sum(jnp.where(gt, 1.0, 0.0), axis=1, keepdims=True)

    nch = t // LANE
    tri = (lax.broadcasted_iota(I32, (LANE, LANE), 0) <= lax.broadcasted_iota(I32, (LANE, LANE), 1))
    tri = jnp.where(tri, 1.0, 0.0).astype(BF16)

    def chunk(a, c):
        return a[:, c * LANE:(c + 1) * LANE]

    off = jnp.zeros((ne, 1), F32)
    sel = []
    for c in range(nch):
        eqc = jnp.where(chunk(eq, c), 1.0, 0.0)
        inc = jnp.dot(eqc.astype(BF16), tri, preferred_element_type=F32)
        rank = inc - eqc + off
        off = off + inc[:, LANE - 1:LANE]
        sel.append(jnp.where(chunk(gt, c) | (chunk(eq, c) & (rank < need)), 1.0, 0.0))

    lane = lax.broadcasted_iota(I32, (ne, LANE), 1)
    offs = jnp.zeros((ne, LANE), F32)
    off = jnp.zeros((ne, 1), F32)
    per_tile = tt // LANE
    for c in range(nch):
        if c % per_tile == 0:
            offs = jnp.where(lane == c // per_tile, off, offs)
        inc = jnp.dot(sel[c].astype(BF16), tri, preferred_element_type=F32)
        pos = inc - sel[c] + off
        off = off + inc[:, LANE - 1:LANE]
        pos_ref[:, c * LANE:(c + 1) * LANE] = jnp.where(sel[c] > 0.0, pos, -1.0).astype(I32)
    offs = jnp.where(lane == nch // per_tile, off, offs)
    off_ref[...] = offs.astype(I32)


def _select(logits_t, t0_blk, t, cap):
    b, ne, _ = logits_t.shape
    tt = min(TOK_TILE, t)
    kern = functools.partial(_select_kernel, cap=cap, tt=tt)
    return pl.pallas_call(
        kern,
        grid=(b,),
        in_specs=[pl.BlockSpec((None, ne, t), lambda i: (i, 0, t0_blk))],
        out_specs=[pl.BlockSpec((None, ne, t), lambda i: (i, 0, 0)),
                   pl.BlockSpec((None, ne, t), lambda i: (i, 0, 0)),
                   pl.BlockSpec((None, ne, LANE), lambda i: (i, 0, 0))],
        out_shape=[jax.ShapeDtypeStruct((b, ne, t), I32),
                   jax.ShapeDtypeStruct((b, ne, t), F32),
                   jax.ShapeDtypeStruct((b, ne, LANE), I32)],
        compiler_params=_cparams("parallel"),
        name="moe_select",
    )(logits_t)


def _window_onehot(posr, lo, cap, w, rows):
    base = pl.multiple_of(jnp.minimum((lo // SUBLANE) * SUBLANE, cap - w), SUBLANE)
    hit = ((posr - base) == rows) & (posr >= lo)
    return base, hit


def _gather_kernel(off_sm, h_ref, pos_ref, aff_ref, xe_ref, gs_ref, acc, gacc, *, cap, tt, nt, w):
    bi, e = pl.program_id(0), pl.program_id(1)
    ne = pl.num_programs(1)
    acc[...] = jnp.zeros_like(acc)
    gacc[...] = jnp.zeros_like(gacc)
    rows = lax.broadcasted_iota(I32, (w, tt), 0)
    obase = (bi * ne + e) * (nt + 1)

    def tile(j, carry):
        p0, p1 = off_sm[obase + j], off_sm[obase + j + 1]
        posr = pos_ref[pl.ds(j, 1), :]
        affr = aff_ref[pl.ds(j, 1), :]
        ht = h_ref[pl.ds(pl.multiple_of(j * tt, tt), tt), :]

        def window(lo):
            base, hit = _window_onehot(posr, lo, cap, w, rows)
            onehot = jnp.where(hit, 1.0, 0.0).astype(BF16)
            acc[pl.ds(base, w), :] += jnp.dot(onehot, ht, preferred_element_type=F32)
            gacc[pl.ds(base, w), :] += jnp.sum(jnp.where(hit, affr, 0.0), axis=1, keepdims=True)
            return base + w

        nxt = window(p0)

        @pl.when(p1 > nxt)
        def _():
            window(nxt)

        return carry

    lax.fori_loop(0, nt, tile, 0)
    xe_ref[...] = acc[...].astype(BF16)
    gs_ref[...] = gacc[...]


def _gather(off_flat, bm, pos4, aff4, t0_blk, t, cap):
    b, _, d = bm.shape
    ne, nt, tt = pos4.shape[1], pos4.shape[2], pos4.shape[3]
    w = min(tt, cap)
    kern = functools.partial(_gather_kernel, cap=cap, tt=tt, nt=nt, w=w)
    grid_spec = pltpu.PrefetchScalarGridSpec(
        num_scalar_prefetch=1,
        grid=(b, ne),
        in_specs=[pl.BlockSpec((None, t, d), lambda i, e, o: (i, t0_blk, 0)),
                  pl.BlockSpec((None, None, nt, tt), lambda i, e, o: (i, e, 0, 0)),
                  pl.BlockSpec((None, None, nt, tt), lambda i, e, o: (i, e, 0, 0))],
        out_specs=[pl.BlockSpec((None, None, cap, d), lambda i, e, o: (e, i, 0, 0)),
                   pl.BlockSpec((None, None, cap, 1), lambda i, e, o: (e, i, 0, 0))],
        scratch_shapes=[pltpu.VMEM((cap, d), F32), pltpu.VMEM((cap, 1), F32)])
    return pl.pallas_call(
        kern,
        grid_spec=grid_spec,
        out_shape=[jax.ShapeDtypeStruct((ne, b, cap, d), BF16),
                   jax.ShapeDtypeStruct((ne, b, cap, 1), F32)],
        compiler_params=_cparams("parallel", "arbitrary"),
        name="moe_gather",
    )(off_flat, bm, pos4, aff4)


def _ffn_kernel(x_ref, gs_ref, w1_ref, w3_ref, w2_ref, o_ref, w1b, w3b, w2b):
    @pl.when(pl.program_id(1) == 0)
    def _():
        w1b[...] = w1_ref[...].astype(BF16)
        w3b[...] = w3_ref[...].astype(BF16)
        w2b[...] = w2_ref[...].astype(BF16)

    x = x_ref[...]
    h1 = jnp.dot(x, w1b[...], preferred_element_type=F32)
    h3 = jnp.dot(x, w3b[...], preferred_element_type=F32)
    hid = (h1 * _sigmoid(h1)) * h3
    o_ref[...] = jnp.dot(hid.astype(BF16), w2b[...], preferred_element_type=F32) * gs_ref[...]


def _ffn(xe, gs, w1, w3, w2, layer):
    ne, r, d = xe.shape
    f = w1.shape[-1]
    tr = min(r, 512)
    return pl.pallas_call(
        _ffn_kernel,
        grid=(ne, r // tr),
        in_specs=[pl.BlockSpec((None, tr, d), lambda e, i: (e, i, 0)),
                  pl.BlockSpec((None, tr, 1), lambda e, i: (e, i, 0)),
                  pl.BlockSpec((None, None, d, f), lambda e, i: (layer, e, 0, 0)),
                  pl.BlockSpec((None, None, d, f), lambda e, i: (layer, e, 0, 0)),
                  pl.BlockSpec((None, None, f, d), lambda e, i: (layer, e, 0, 0))],
        out_specs=pl.BlockSpec((None, tr, d), lambda e, i: (e, i, 0)),
        out_shape=jax.ShapeDtypeStruct((ne, r, d), F32),
        scratch_shapes=[pltpu.VMEM((d, f), BF16), pltpu.VMEM((d, f), BF16), pltpu.VMEM((f, d), BF16)],
        compiler_params=_cparams("parallel", "arbitrary"),
        name="moe_ffn",
    )(xe, gs, w1, w3, w2)


def _combine_kernel(off_sm, x1_ref, mod_ref, pos_ref, ye_ref, o_ref, *, cap, tt, nt, w, tiles_per_blk):
    bi, ci, e = pl.program_id(0), pl.program_id(1), pl.program_id(2)
    ne = pl.num_programs(2)

    @pl.when(e == 0)
    def _():
        o_ref[...] = jnp.zeros_like(o_ref)

    rows = lax.broadcasted_iota(I32, (w, tt), 0)
    obase = (bi * ne + e) * (nt + 1)

    def tile(jj, carry):
        j = ci * tiles_per_blk + jj
        p0, p1 = off_sm[obase + j], off_sm[obase + j + 1]
        posr = pos_ref[pl.ds(j, 1), :]
        r0 = pl.multiple_of(jj * tt, tt)

        def window(lo):
            base, hit = _window_onehot(posr, lo, cap, w, rows)
            onehot = jnp.where(hit, 1.0, 0.0).astype(BF16)
            hi, lo_part = _split2(ye_ref[pl.ds(base, w), :])
            d = functools.partial(lax.dot_general, dimension_numbers=_TN, preferred_element_type=F32)
            o_ref[pl.ds(r0, tt), :] += d(onehot, hi) + d(onehot, lo_part)
            return base + w

        nxt = window(p0)

        @pl.when(p1 > nxt)
        def _():
            window(nxt)

        return carry

    lax.fori_loop(0, tiles_per_blk, tile, 0)

    @pl.when(e == ne - 1)
    def _():
        o_ref[...] = x1_ref[...] + mod_ref[...][5:6] * o_ref[...]


def _combine(off_flat, x1, mod, stream, pos4, ye, t0_blk, t, cap):
    b, s, d = x1.shape
    ne, nt, tt = pos4.shape[1], pos4.shape[2], pos4.shape[3]
    w = min(tt, cap)
    tc = min(t, 2048)
    nblk = t // tc
    kern = functools.partial(_combine_kernel, cap=cap, tt=tt, nt=nt, w=w, tiles_per_blk=tc // tt)
    grid_spec = pltpu.PrefetchScalarGridSpec(
        num_scalar_prefetch=1,
        grid=(b, nblk, ne),
        in_specs=[pl.BlockSpec((None, tc, d), lambda i, c, e, o: (i, t0_blk * nblk + c, 0)),
                  pl.BlockSpec((None, None, 6, d), lambda i, c, e, o: (i, stream, 0, 0)),
                  pl.BlockSpec((None, None, nt, tt), lambda i, c, e, o: (i, e, 0, 0)),
                  pl.BlockSpec((None, None, cap, d), lambda i, c, e, o: (e, i, 0, 0))],
        out_specs=pl.BlockSpec((None, tc, d), lambda i, c, e, o: (i, t0_blk * nblk + c, 0)))
    return pl.pallas_call(
        kern,
        grid_spec=grid_spec,
        out_shape=jax.ShapeDtypeStruct((b, s, d), F32),
        input_output_aliases={1: 0},
        compiler_params=_cparams("parallel", "parallel", "arbitrary"),
        name="moe_combine",
    )(off_flat, x1, mod, pos4, ye)


def _moe_stream(x1, bm, logits_t, mod, stream, t0_blk, t, experts):
    b = x1.shape[0]
    cap = CAPACITY_FACTOR * t // N_EXPERTS
    pos, aff, offs = _select(logits_t, t0_blk, t, cap)
    tt = min(TOK_TILE, t)
    nt = t // tt
    pos4 = pos.reshape(b, N_EXPERTS, nt, tt)
    aff4 = aff.reshape(b, N_EXPERTS, nt, tt)
    off_flat = offs[:, :, :nt + 1].reshape(-1)
    xe, gs = _gather(off_flat, bm, pos4, aff4, t0_blk, t, cap)
    d = x1.shape[-1]
    ye = _ffn(xe.reshape(N_EXPERTS, b * cap, d), gs.reshape(N_EXPERTS, b * cap, 1), *experts)
    return _combine(off_flat, x1, mod, stream, pos4, ye.reshape(N_EXPERTS, b, cap, d), t0_blk, t, cap)


def _rwkv_stream_kernel(*refs, t_lat, has_vres):
    if has_vres:
        (x_ref, xp_ref, xn_ref, mod_ref, g1_ref, mu_ref, wr_ref, wk_ref, wv_ref, w1_ref, w2_ref, w0_ref,
         a1_ref, a2_ref, a0_ref, gg1_ref, gg2_ref, kk_ref, ka_ref, rk_ref, seg_ref, segt_ref,
         vf_ref, v0_ref, v1_ref, v2_ref,
         r_out, v_out, kk_out, lw_out, kd_out, b_out, g_out, bonus_out) = refs
    else:
        (x_ref, xp_ref, xn_ref, mod_ref, g1_ref, mu_ref, wr_ref, wk_ref, wv_ref, w1_ref, w2_ref, w0_ref,
         a1_ref, a2_ref, a0_ref, gg1_ref, gg2_ref, kk_ref, ka_ref, rk_ref, seg_ref, segt_ref,
         r_out, v_out, kk_out, lw_out, kd_out, b_out, g_out, bonus_out) = refs
    j = pl.program_id(1)
    mod = mod_ref[...]
    g1 = g1_ref[...]
    nm = lambda x: _norm_mod(x, g1, mod[1:2], mod[0:1])
    h = nm(x_ref[...])
    tm, d = h.shape
    hp = nm(xp_ref[...])[SUBLANE - 1:SUBLANE]
    hn = nm(xn_ref[...])[0:1]
    row = lax.broadcasted_iota(I32, (tm, 1), 0)
    grow = row + j * tm
    s_tot = pl.num_programs(1) * tm
    first = (grow == 0) | (grow == t_lat)
    last = (grow == t_lat - 1) | (grow == s_tot - 1)
    prev = jnp.where(row == 0, hp, pltpu.roll(h, 1, 0))
    prev = jnp.where(first, 0.0, prev)
    nxt = jnp.where(row == tm - 1, hn, pltpu.roll(h, tm - 1, 0))
    nxt = jnp.where(last, 0.0, nxt)
    xx = 0.5 * (prev + nxt) - h
    mu = mu_ref[...]
    xs = [h + xx * mu[m:m + 1] for m in range(6)]
    r = _bdot(xs[0], wr_ref[...])
    k = _bdot(xs[1], wk_ref[...])
    v = _bdot(xs[2], wv_ref[...])
    if has_vres:
        gate = _sigmoid(v0_ref[...] + _bdot(_bdot(xs[2], v1_ref[...]), v2_ref[...]))
        v = v + (vf_ref[...] - v) * gate
    wz = w0_ref[...] + _bdot(jnp.tanh(_bdot(xs[3], w1_ref[...])), w2_ref[...])
    lw = -math.exp(-0.5) * _sigmoid(wz)
    a = _sigmoid(a0_ref[...] + _bdot(_bdot(xs[4], a1_ref[...]), a2_ref[...]))
    g = _bdot(_sigmoid(_bdot(xs[5], gg1_ref[...])), gg2_ref[...])
    seg, segt = seg_ref[...], segt_ref[...]
    segsum = lambda t: _dot_exact_rhs01(_dot_exact_rhs01(t, seg), segt)
    kkr = k * kk_ref[...]
    kk = kkr / jnp.maximum(jnp.sqrt(segsum(kkr * kkr)), 1e-12)
    ka = ka_ref[...]
    kd0 = k * (1.0 + (a[:, :d] - 1.0) * ka)
    kd1 = k * (1.0 + (a[:, d:] - 1.0) * ka)
    r_out[...] = r
    v_out[...] = v
    kk_out[...] = kk
    lw_out[...] = lw
    kd_out[:, :d] = kd0
    kd_out[:, d:] = kd1
    b_out[:, :d] = kk * a[:, :d]
    b_out[:, d:] = kk * a[:, d:]
    g_out[...] = g
    bonus_out[...] = segsum(r * (kd0 + kd1) * rk_ref[...]) * v


def _rwkv_stream(x, mod, g1, wts, vres, t_lat):
    b, s, d = x.shape
    tm = TOK_TILE
    nlt = t_lat // tm
    nsub = tm // SUBLANE
    const = lambda a: pl.BlockSpec(a.shape, lambda i, j: (0,) * a.ndim)
    tok = lambda w: pl.BlockSpec((None, tm, w), lambda i, j: (i, j, 0))
    in_specs = [tok(d),
                pl.BlockSpec((None, SUBLANE, d), lambda i, j: (i, jnp.maximum(j * nsub - 1, 0), 0)),
                pl.BlockSpec((None, SUBLANE, d), lambda i, j: (i, jnp.minimum((j + 1) * nsub, s // SUBLANE - 1), 0)),
                pl.BlockSpec((None, None, 6, d), lambda i, j: (i, (j >= nlt).astype(I32), 0, 0)),
                const(g1)] + [const(a) for a in wts]
    args = [x, x, x, mod, g1] + list(wts)
    if vres is not None:
        vf, v0, v1, v2 = vres
        in_specs += [tok(d), const(v0), const(v1), const(v2)]
        args += [vf, v0, v1, v2]
    kern = functools.partial(_rwkv_stream_kernel, t_lat=t_lat, has_vres=vres is not None)
    widths = [d, d, d, 2 * d, 2 * d, 2 * d, d, d]
    return pl.pallas_call(
        kern,
        grid=(b, s // tm),
        in_specs=in_specs,
        out_specs=[tok(w) for w in widths],
        out_shape=[jax.ShapeDtypeStruct((b, s, w), F32) for w in widths],
        compiler_params=_cparams("parallel", "parallel"),
        name="rwkv_stream",
    )(*args)


def _scan_kernel(r_ref, v_ref, kk_ref, lw_ref, kd_ref, b_ref, y_ref, h_ref, *, reverse, nchunk):
    L = SCAN_CHUNK

    @pl.when(pl.program_id(2) == 0)
    def _():
        h_ref[...] = jnp.zeros_like(h_ref)

    n2 = 2 * L
    rr = lax.broadcasted_iota(I32, (n2, n2), 0)
    cc = lax.broadcasted_iota(I32, (n2, n2), 1)
    tr, tc = rr % L, cc % L
    same = (rr // L) == (cc // L)
    before = (tr < tc) if reverse else (tr > tc)
    strict = same & before
    incl = same & (before | (tr == tc))
    eye = rr == cc
    blk16 = (rr // 16) == (cc // 16)
    blk32 = (rr // 32) == (cc // 32)
    ri = lax.broadcasted_iota(I32, (L, L), 0)
    ci = lax.broadcasted_iota(I32, (L, L), 1)
    tri = jnp.where((ci >= ri) if reverse else (ci <= ri), 1.0, 0.0).astype(BF16)
    head0 = lax.broadcasted_iota(I32, (L, LANE), 1) < RW_HEAD

    def stack(x):
        return jnp.concatenate([jnp.where(head0, x, 0.0), jnp.where(head0, 0.0, x)], axis=0)

    order = list(range(nchunk - 1, -1, -1) if reverse else range(nchunk))
    npp = y_ref.shape[-1] // LANE
    units = [(pp, ch) for pp in range(npp) for ch in order]

    def each(f, *lists):
        return [f(*a) for a in zip(*lists)]

    def ld(ref):
        return [ref[pl.ds(ch * L, L), pl.ds(pp * LANE, LANE)] for pp, ch in units]

    dd = functools.partial(jnp.dot, preferred_element_type=F32)
    lw = ld(lw_ref)
    lws = each(_split3, lw)
    cs = each(lambda t: dd(tri, t[0]) + (dd(tri, t[1]) + dd(tri, t[2])), lws)
    total = each(lambda c: c[0:1] if reverse else c[L - 1:L], cs)
    gam = each(jnp.exp, cs)
    ginv = each(lambda c: jnp.exp(-c), cs)
    gprev = each(lambda c, l: jnp.exp(c - l), cs, lw)
    gend = each(lambda t, c: jnp.exp(t - c), total, cs)
    kk, bb, kd, r, v = ld(kk_ref), ld(b_ref), ld(kd_ref), ld(r_ref), ld(v_ref)
    mul_stack = lambda a, g: stack(a * g)
    rtm = each(mul_stack, r, gam)
    rts = each(_split2, rtm)
    kkm, btm, ktm = (each(_split2, each(mul_stack, a, g)) for a, g in ((kk, gprev), (bb, ginv), (kd, ginv)))
    bht, kht = (each(lambda a, g: _split2(stack(a * g).T), a, gend) for a in (bb, kd))
    vm = each(_split2, each(stack, v))
    n_ab = each(lambda a, c, e: _mm3_rows([a, c], e, _NT), kkm, rts, btm)
    m_ak = each(lambda a, c, e: _mm3_rows([a, c], e, _NT), kkm, rts, ktm)
    n = each(lambda t: jnp.where(strict, t[0], 0.0), n_ab)
    ab = each(lambda t: jnp.where(incl, t[1], 0.0), n_ab)
    m = each(lambda t: jnp.where(strict, t[0], 0.0), m_ak)
    ak = each(lambda t: jnp.where(incl, t[1], 0.0), m_ak)
    nd = each(lambda t: jnp.where(blk16, t, 0.0), n)
    x = each(lambda t: jnp.where(eye, 1.0, 0.0) - t, nd)
    pw = nd
    for _ in range(3):
        pw = each(lambda t: _bdot(t, t), pw)
        x = each(lambda a, c: a + _bdot(a, c), x, pw)
    ident = jnp.where(eye, 1.0, 0.0)
    for inner, outer in ((blk16, blk32), (blk32, same)):
        c = each(lambda t: jnp.where(outer & jnp.logical_not(inner), t, 0.0), n)
        e = each(lambda a: a - ident, x)
        ec = each(_bdot, e, c)
        x = each(lambda a, t, u, s: a - t - (u + _bdot(t + u, s)), x, c, ec, e)
    xs = each(_split2, x)
    abs_ = each(_split2, ab)
    mv_akv_khv = each(lambda a, c, e, f: _mm3_rows([_split2(a), _split2(c), e], f), m, ak, kht, vm)
    cat = lambda a, c: tuple(jnp.concatenate([s, t], axis=1) for s, t in zip(a, c))
    wu = each(lambda a, c, t: _split2(_mm3(a, cat(c, _split2(t[0])))), xs, kkm, mv_akv_khv)
    bw_abw = each(lambda a, c, e: _mm3_rows([a, c], e), bht, abs_, wu)
    p = each(lambda t, a: jnp.where(eye, jnp.exp(t), 0.0) - a[0][:, :LANE], total, bw_abw)
    q = each(lambda t, a: t[2] - a[0][:, LANE:], mv_akv_khv, bw_abw)
    rres = each(lambda t, a: t - a[1][:, :LANE], rtm, bw_abw)
    y0 = each(lambda t, a: t[1] - a[1][:, LANE:], mv_akv_khv, bw_abw)
    for i, (pp, ch) in enumerate(units):
        y, hn = _mm3_rows([_split2(rres[i]), _split2(p[i])], _split2(h_ref[pp]))
        y = y + y0[i]
        y_ref[pl.ds(ch * L, L), pl.ds(pp * LANE, LANE)] = y[:L] + y[L:]
        h_ref[pp] = hn + q[i]


def _scan(r, v, kk, lw, kd, bb, t_lat, reverse):
    b, s, d = r.shape
    tb = TOK_TILE
    nl, nc = t_lat // tb, (s - t_lat) // tb
    z = 1 if reverse else 0

    def blk(c):
        if reverse:
            return jnp.where(c < nc, nl + nc - 1 - c, nl - 1 - (c - nc))
        return jnp.where(c < nc, nl + c, c - nc)

    wl = SCAN_PAIRS * LANE
    ngrp = d // wl
    shared = pl.BlockSpec((None, tb, wl), lambda i, p, c: (i, blk(c), p))
    dirn = pl.BlockSpec((None, tb, wl), lambda i, p, c: (i, blk(c), z * ngrp + p))
    kern = functools.partial(_scan_kernel, reverse=reverse, nchunk=tb // SCAN_CHUNK)
    return pl.pallas_call(
        kern,
        grid=(b, ngrp, nl + nc),
        in_specs=[shared, shared, shared, dirn, dirn, dirn],
        out_specs=pl.BlockSpec((None, tb, wl), lambda i, p, c: (i, blk(c), p)),
        out_shape=jax.ShapeDtypeStruct((b, s, d), F32),
        scratch_shapes=[pltpu.VMEM((SCAN_PAIRS, LANE, LANE), F32)],
        compiler_params=_cparams("parallel", "parallel", "arbitrary"),
        name="wkv_scan_bwd" if reverse else "wkv_scan_fwd",
    )(r, v, kk, lw, kd, bb)


def _rwkv_out_kernel(yf_ref, yb_ref, bonus_ref, g_ref, lg_ref, lb_ref, seg_ref, segt_ref, o_ref):
    seg, segt = seg_ref[...], segt_ref[...]
    segmean = lambda t: _dot_exact_rhs01(_dot_exact_rhs01(t, seg), segt) * (1.0 / RW_HEAD)
    y = yf_ref[...] + yb_ref[...]
    dlt = y - segmean(y)
    var = segmean(dlt * dlt)
    yn = (dlt * lax.rsqrt(var + RW_LNX_EPS)) * lg_ref[...] + lb_ref[...]
    o_ref[...] = ((yn + bonus_ref[...]) * g_ref[...]).astype(BF16)


def _rwkv_out(yf, yb, bonus, g, lnx_g, lnx_b, seg, segt):
    b, s, d = yf.shape
    tm = TOK_TILE
    tok = pl.BlockSpec((None, tm, d), lambda i, j: (i, j, 0))
    const = lambda a: pl.BlockSpec(a.shape, lambda i, j: (0,) * a.ndim)
    return pl.pallas_call(
        _rwkv_out_kernel,
        grid=(b, s // tm),
        in_specs=[tok, tok, tok, tok, const(lnx_g), const(lnx_b), const(seg), const(segt)],
        out_specs=tok,
        out_shape=jax.ShapeDtypeStruct((b, s, d), BF16),
        compiler_params=_cparams("parallel", "parallel"),
        name="rwkv_out",
    )(yf, yb, bonus, g, lnx_g, lnx_b, seg, segt)


def _final_kernel(x_ref, g_ref, o_ref):
    o_ref[...] = _rms(x_ref[...], g_ref[...])


def _final_norm(x, g, t_lat):
    b, _, d = x.shape
    tm = TOK_TILE
    return pl.pallas_call(
        _final_kernel,
        grid=(b, t_lat // tm),
        in_specs=[pl.BlockSpec((None, tm, d), lambda i, j: (i, j, 0)),
                  pl.BlockSpec(g.shape, lambda i, j: (0, 0))],
        out_specs=pl.BlockSpec((None, tm, d), lambda i, j: (i, j, 0)),
        out_shape=jax.ShapeDtypeStruct((b, t_lat, d), F32),
        compiler_params=_cparams("parallel", "parallel"),
        name="final_norm",
    )(x, g)


def _rope_tables(t_lat, t_ctx):
    rows = t_lat // GRID_W
    r = jnp.broadcast_to(jnp.arange(rows)[:, None], (rows, GRID_W)).reshape(-1).astype(F32)
    col = jnp.broadcast_to(jnp.arange(GRID_W)[None, :], (rows, GRID_W)).reshape(-1).astype(F32)
    n_freq = MLA_ROPE // 4
    inv = ROPE_THETA ** (-jnp.arange(n_freq, dtype=F32) / n_freq)
    ang = jnp.concatenate([r[:, None] * inv, col[:, None] * inv], axis=-1)
    cos, sin = jnp.cos(ang), jnp.sin(ang)
    cosf = jnp.concatenate([cos, cos, cos, cos], axis=-1)
    sins = jnp.concatenate([-sin, sin, -sin, sin], axis=-1)
    cosf = jnp.concatenate([cosf, jnp.ones((t_ctx, LANE), F32)], axis=0)
    sins = jnp.concatenate([sins, jnp.zeros((t_ctx, LANE), F32)], axis=0)
    return cosf, sins


def _blockdiag2(w):
    z = jnp.zeros_like(w[0])
    return jnp.concatenate([jnp.concatenate([w[0], z], axis=1), jnp.concatenate([z, w[1]], axis=1)], axis=0)


def kernel(x, c, ctx, c_ctx, mod_w, mod_b, norm1_g, norm2_g, final_g, mla_w_in, mla_g_q, mla_g_kv, mla_w_uq, mla_w_ukv, mla_w_o, rw_mu, rw_w_rkv, rw_w0, rw_w1, rw_w2, rw_a0, rw_a1, rw_a2, rw_v0, rw_v1, rw_v2, rw_g1, rw_g2, rw_k_k, rw_k_a, rw_r_k, rw_lnx_g, rw_lnx_b, rw_w_o, moe_router, moe_w1, moe_w3, moe_w2):
    b, t_lat, d = x.shape
    t_ctx = ctx.shape[1]
    depth = mod_w.shape[0]
    assert b + 1 <= SUBLANE and t_lat % TOK_TILE == 0 and t_ctx % TOK_TILE == 0 and d % LANE == 0
    n_lat_tiles = t_lat // TOK_TILE
    ctx_blk = t_lat // t_ctx
    assert ctx_blk * t_ctx == t_lat

    cc = jnp.concatenate([c, c_ctx[None], jnp.zeros((SUBLANE - b - 1, d), F32)], axis=0)
    mods = _modulation(cc, mod_w, mod_b).reshape(depth, SUBLANE, 6, d)
    mod_all = jnp.stack([mods[:, :b], jnp.broadcast_to(mods[:, b:b + 1], (depth, b, 6, d))], axis=2)

    xs = jnp.concatenate([x, ctx], axis=1)
    cosf, sins = _rope_tables(t_lat, t_ctx)
    row = lambda a: a.reshape(1, -1)
    seg = (jnp.arange(d)[:, None] // RW_HEAD == jnp.arange(LANE)[None, :]).astype(BF16)
    segt = seg.T

    v_first = None
    for layer in range(depth):
        need_ctx = layer < depth - 1
        mod = mod_all[layer]
        j = layer // 2
        if layer % 2 == 0:
            win = jnp.pad(mla_w_in[j], ((0, 0), (0, 4 * LANE - mla_w_in.shape[-1]))).astype(BF16)
            wuq = mla_w_uq[j].reshape(MLA_Q_LORA, MLA_HEADS, MLA_NOPE + MLA_ROPE)
            wuq = jnp.concatenate([wuq[..., :MLA_NOPE].reshape(MLA_Q_LORA, -1),
                                   wuq[..., MLA_NOPE:].reshape(MLA_Q_LORA, -1)], axis=1).astype(BF16)
            q, k, v = _mla_proj(xs, mod, row(norm1_g[layer]), win, row(mla_g_q[j]), row(mla_g_kv[j]),
                                wuq, mla_w_ukv[j].astype(BF16), cosf, sins, n_lat_tiles)
            y = _attention(q, k, v, t_lat)
            wo = mla_w_o[j].astype(BF16)
        else:
            lora = lambda w: jnp.concatenate([w[0], w[1]], axis=1).astype(BF16)
            wts = [rw_mu[j], rw_w_rkv[j, 0].astype(BF16), rw_w_rkv[j, 1].astype(BF16), rw_w_rkv[j, 2].astype(BF16),
                   lora(rw_w1[j]), _blockdiag2(rw_w2[j]).astype(BF16), rw_w0[j].reshape(1, -1),
                   lora(rw_a1[j]), _blockdiag2(rw_a2[j]).astype(BF16), rw_a0[j].reshape(1, -1),
                   rw_g1[j].astype(BF16), rw_g2[j].astype(BF16),
                   row(rw_k_k[j]), row(rw_k_a[j]), row(rw_r_k[j]), seg, segt]
            vres = None
            if j > 0:
                pad = LANE - rw_v1.shape[-1]
                vres = (v_first, row(rw_v0[j - 1]), jnp.pad(rw_v1[j - 1], ((0, 0), (0, pad))).astype(BF16),
                        jnp.pad(rw_v2[j - 1], ((0, pad), (0, 0))).astype(BF16))
            r, v, kk, lw, kd, bb, g, bonus = _rwkv_stream(xs, mod, row(norm1_g[layer]), wts, vres, t_lat)
            if j == 0:
                v_first = v
            yf = _scan(r, v, kk, lw, kd, bb, t_lat, reverse=False)
            yb = _scan(r, v, kk, lw, kd, bb, t_lat, reverse=True)
            y = _rwkv_out(yf, yb, bonus, g, row(rw_lnx_g[j]), row(rw_lnx_b[j]), seg, segt)
            wo = rw_w_o[j].astype(BF16)
        x1, bm, logits_t = _post_mixer(xs, y, wo, mod, row(norm2_g[layer]), moe_router[layer].T.astype(BF16),
                                       n_lat_tiles)
        experts = (moe_w1, moe_w3, moe_w2, layer)
        xs = _moe_stream(x1, bm, logits_t, mod, 0, 0, t_lat, experts)
        if need_ctx:
            xs = _moe_stream(xs, bm, logits_t, mod, 1, ctx_blk, t_ctx, experts)
    return _final_norm(xs, row(final_g), t_lat)
```

```python
import functools
import math

import jax
import jax.numpy as jnp
from jax import lax
from jax.experimental import pallas as pl
from jax.experimental.pallas import tpu as pltpu

F32, BF16, I32 = jnp.float32, jnp.bfloat16, jnp.int32

GRID_W = 64
MLA_HEADS, MLA_NOPE, MLA_ROPE, MLA_V = 8, 128, 64, 128
MLA_Q_LORA, MLA_KV_LORA = 256, 128
MLA_SCALE = (MLA_NOPE + MLA_ROPE) ** -0.5
ROPE_THETA = 10000.0
RW_HEAD = 64
RW_LNX_EPS = 64e-5
N_EXPERTS = 16
CAPACITY_FACTOR = 2
RMS_EPS = 1e-6

LANE = 128
SUBLANE = 8
TOK_TILE = 256
WIN_ALIGN = 16
ATTN_TILE = 1024
ATTN_ROWS = 256
SCAN_CHUNK = 64
SCAN_PAIRS = 4
VMEM_LIMIT = 56 * 1024 * 1024


def _cparams(*sem):
    return pltpu.CompilerParams(dimension_semantics=sem, vmem_limit_bytes=VMEM_LIMIT)


def _bdot(a, b):
    return jnp.dot(a.astype(BF16), b.astype(BF16), preferred_element_type=F32)


_NT = (((1,), (1,)), ((), ()))
_TN = (((0,), (0,)), ((), ()))
_NN = (((1,), (0,)), ((), ()))


def _split2(x):
    hi = x.astype(BF16)
    lo = (x - hi.astype(F32)).astype(BF16)
    return hi, lo


def _split3(x):
    hi = x.astype(BF16)
    r = x - hi.astype(F32)
    mid = r.astype(BF16)
    lo = (r - mid.astype(F32)).astype(BF16)
    return hi, mid, lo


def _mm3(a, b, dims=_NN):
    d = functools.partial(lax.dot_general, dimension_numbers=dims, preferred_element_type=F32)
    return d(a[0], b[0]) + (d(a[0], b[1]) + d(a[1], b[0]))


def _mm3_rows(lhs, b, dims=_NN):
    his = jnp.concatenate([a[0] for a in lhs], axis=0)
    los = jnp.concatenate([a[1] for a in lhs], axis=0)
    d = functools.partial(lax.dot_general, dimension_numbers=dims, preferred_element_type=F32)
    nr = his.shape[0]
    big = d(jnp.concatenate([his, los], axis=0), b[0])
    res = big[:nr] + (d(his, b[1]) + big[nr:])
    outs, o = [], 0
    for a in lhs:
        outs.append(res[o:o + a[0].shape[0]])
        o += a[0].shape[0]
    return outs


def _dot_exact_rhs01(x, m01):
    h, m, l = _split3(x)
    d = functools.partial(jnp.dot, preferred_element_type=F32)
    return d(h, m01) + (d(m, m01) + d(l, m01))


def _rms(x, g):
    return (x * lax.rsqrt(jnp.mean(x * x, axis=-1, keepdims=True) + RMS_EPS)) * g


def _norm_mod(x, g, scale, shift):
    return _rms(x, g) * (1.0 + scale) + shift


def _sigmoid(x):
    return 1.0 / (1.0 + jnp.exp(-x))


def _mod_kernel(c_ref, w_ref, b_ref, o_ref):
    s = c_ref[...]
    s = s * _sigmoid(s)
    o_ref[...] = _bdot(s, w_ref[...]) + b_ref[...]


def _modulation(cc, mod_w, mod_b):
    depth, d, n = mod_w.shape
    tn = n // 4
    return pl.pallas_call(
        _mod_kernel,
        grid=(depth, n // tn),
        in_specs=[pl.BlockSpec((SUBLANE, d), lambda l, j: (0, 0)),
                  pl.BlockSpec((None, d, tn), lambda l, j: (l, 0, j)),
                  pl.BlockSpec((None, 1, tn), lambda l, j: (l, 0, j))],
        out_specs=pl.BlockSpec((None, SUBLANE, tn), lambda l, j: (l, 0, j)),
        out_shape=jax.ShapeDtypeStruct((depth, SUBLANE, n), F32),
        compiler_params=_cparams("arbitrary", "arbitrary"),
        name="modulation",
    )(cc, mod_w, mod_b.reshape(depth, 1, n))


def _rope128(x, cosf, sins):
    lane = lax.broadcasted_iota(I32, x.shape, 1)
    first = (lane % MLA_ROPE) < (MLA_ROPE // 2)
    rot = jnp.where(first, pltpu.roll(x, LANE - MLA_ROPE // 2, 1), pltpu.roll(x, MLA_ROPE // 2, 1))
    return x * cosf + rot * sins


def _mla_proj_kernel(x_ref, mod_ref, g1_ref, win_ref, gq_ref, gkv_ref, wuq_ref, wukv_ref,
                     cos_ref, sin_ref, q_ref, k_ref, v_ref):
    mod = mod_ref[...]
    h = _norm_mod(x_ref[...], g1_ref[...], mod[1:2], mod[0:1])
    z = _bdot(h, win_ref[...])
    cq = _rms(z[:, :MLA_Q_LORA], gq_ref[...])
    ckv = _rms(z[:, MLA_Q_LORA:MLA_Q_LORA + MLA_KV_LORA], gkv_ref[...])
    kr = z[:, MLA_Q_LORA + MLA_KV_LORA:]
    q = _bdot(cq, wuq_ref[...])
    kv = _bdot(ckv, wukv_ref[...])
    cosf, sins = cos_ref[...], sin_ref[...]
    kr_lo = _rope128(kr, cosf, sins)
    kr_hi = pltpu.roll(kr_lo, MLA_ROPE, 1)
    nope_w = MLA_HEADS * MLA_NOPE
    for g in range(MLA_HEADS // 2):
        qr = _rope128(q[:, nope_w + LANE * g:nope_w + LANE * (g + 1)], cosf, sins)
        for hh in (2 * g, 2 * g + 1):
            qn = q[:, MLA_NOPE * hh:MLA_NOPE * (hh + 1)]
            q_ref[hh] = jnp.concatenate([qn, qr], axis=1).astype(BF16)
    for hh in range(MLA_HEADS):
        base = (MLA_NOPE + MLA_V) * hh
        krh = kr_lo if hh % 2 == 0 else kr_hi
        k_ref[hh] = jnp.concatenate([kv[:, base:base + MLA_NOPE], krh], axis=1).astype(BF16)
        v_ref[hh] = kv[:, base + MLA_NOPE:base + MLA_NOPE + MLA_V].astype(BF16)


def _mla_proj(x, mod, g1, win, gq, gkv, wuq, wukv, cosf, sins, n_lat_tiles):
    b, s, d = x.shape
    tm = TOK_TILE
    const = lambda shape: pl.BlockSpec(shape, lambda i, j: (0,) * len(shape))
    hspec = lambda w: pl.BlockSpec((None, MLA_HEADS, tm, w), lambda i, j: (i, 0, j, 0))
    return pl.pallas_call(
        _mla_proj_kernel,
        grid=(b, s // tm),
        in_specs=[pl.BlockSpec((None, tm, d), lambda i, j: (i, j, 0)),
                  pl.BlockSpec((None, None, 6, d), lambda i, j: (i, (j >= n_lat_tiles).astype(I32), 0, 0)),
                  const(g1.shape), const(win.shape), const(gq.shape), const(gkv.shape),
                  const(wuq.shape), const(wukv.shape),
                  pl.BlockSpec((tm, LANE), lambda i, j: (j, 0)),
                  pl.BlockSpec((tm, LANE), lambda i, j: (j, 0))],
        out_specs=[hspec(2 * LANE), hspec(2 * LANE), hspec(MLA_V)],
        out_shape=[jax.ShapeDtypeStruct((b, MLA_HEADS, s, 2 * LANE), BF16),
                   jax.ShapeDtypeStruct((b, MLA_HEADS, s, 2 * LANE), BF16),
                   jax.ShapeDtypeStruct((b, MLA_HEADS, s, MLA_V), BF16)],
        compiler_params=_cparams("parallel", "parallel"),
        name="mla_proj",
    )(x, mod, g1, win, gq, gkv, wuq, wukv, cosf, sins)


def _attn_kernel(q_ref, k_ref, v_ref, o_ref, *, n_lat_tiles, t_lat):
    j = pl.program_id(2)
    t_ctx = k_ref.shape[0] - t_lat

    def softmax_pv(s, v):
        m = jnp.max(s, axis=-1, keepdims=True)
        p = jnp.exp2((s - m) * (MLA_SCALE * math.log2(math.e)))
        l = jnp.sum(p, axis=-1, keepdims=True)
        o = jnp.dot(p.astype(BF16), v, preferred_element_type=F32)
        return (o / l).astype(o_ref.dtype)

    @pl.when(j < n_lat_tiles)
    def _():
        for r0 in range(0, o_ref.shape[0], ATTN_ROWS):
            s = lax.dot_general(q_ref[r0:r0 + ATTN_ROWS, :], k_ref[...], _NT, preferred_element_type=F32)
            o_ref[r0:r0 + ATTN_ROWS, :] = softmax_pv(s, v_ref[...])

    @pl.when(j >= n_lat_tiles)
    def _():
        s = lax.dot_general(q_ref[:t_ctx, :], k_ref[t_lat:, :], _NT, preferred_element_type=F32)
        o_ref[:t_ctx, :] = softmax_pv(s, v_ref[t_lat:, :])


def _attention(q, k, v, t_lat):
    b, nh, s, _ = q.shape
    tq = min(ATTN_TILE, t_lat)
    assert t_lat % tq == 0 and s - t_lat <= tq
    kern = functools.partial(_attn_kernel, n_lat_tiles=t_lat // tq, t_lat=t_lat)
    return pl.pallas_call(
        kern,
        grid=(b, nh, pl.cdiv(s, tq)),
        in_specs=[pl.BlockSpec((None, None, tq, q.shape[-1]), lambda i, h, j: (i, h, j, 0)),
                  pl.BlockSpec((None, None, s, k.shape[-1]), lambda i, h, j: (i, h, 0, 0)),
                  pl.BlockSpec((None, None, s, v.shape[-1]), lambda i, h, j: (i, h, 0, 0))],
        out_specs=pl.BlockSpec((None, tq, MLA_V), lambda i, h, j: (i, j, h)),
        out_shape=jax.ShapeDtypeStruct((b, s, nh * MLA_V), BF16),
        compiler_params=_cparams("parallel", "parallel", "arbitrary"),
        name="mla_attention",
    )(q, k, v)


def _post_mixer_kernel(x_ref, y_ref, wo_ref, mod_ref, g2_ref, rt_ref, x1_ref, bm_ref, lg_ref):
    mod = mod_ref[...]
    x1 = x_ref[...] + mod[2:3] * jnp.dot(y_ref[...], wo_ref[...], preferred_element_type=F32)
    x1_ref[...] = x1
    bm = _norm_mod(x1, g2_ref[...], mod[4:5], mod[3:4]).astype(BF16)
    bm_ref[...] = bm
    lg_ref[...] = lax.dot_general(rt_ref[...], bm, _NT, preferred_element_type=F32)


def _post_mixer(x, y, wo, mod, g2, router_t, n_lat_tiles):
    b, s, d = x.shape
    tm = TOK_TILE
    const = lambda shape: pl.BlockSpec(shape, lambda i, j: (0,) * len(shape))
    return pl.pallas_call(
        _post_mixer_kernel,
        grid=(b, s // tm),
        in_specs=[pl.BlockSpec((None, tm, d), lambda i, j: (i, j, 0)),
                  pl.BlockSpec((None, tm, y.shape[-1]), lambda i, j: (i, j, 0)),
                  const(wo.shape),
                  pl.BlockSpec((None, None, 6, d), lambda i, j: (i, (j >= n_lat_tiles).astype(I32), 0, 0)),
                  const(g2.shape), const(router_t.shape)],
        out_specs=[pl.BlockSpec((None, tm, d), lambda i, j: (i, j, 0)),
                   pl.BlockSpec((None, tm, d), lambda i, j: (i, j, 0)),
                   pl.BlockSpec((None, N_EXPERTS, tm), lambda i, j: (i, 0, j))],
        out_shape=[jax.ShapeDtypeStruct((b, s, d), F32),
                   jax.ShapeDtypeStruct((b, s, d), BF16),
                   jax.ShapeDtypeStruct((b, N_EXPERTS, s), F32)],
        compiler_params=_cparams("parallel", "parallel"),
        name="post_mixer",
    )(x, y, wo, mod, g2, router_t)


def _select_kernel(lg_ref, pos_ref, aff_ref, off_ref, *, cap, tt):
    lg = lg_ref[...]
    ne, t = lg.shape
    m = jnp.max(lg, axis=0, keepdims=True)
    ex = jnp.exp(lg - m)
    aff = ex / jnp.sum(ex, axis=0, keepdims=True)
    aff_ref[...] = aff
    bits = pltpu.bitcast(aff, I32)

    def search(i, prefix):
        cand = prefix | jnp.left_shift(jnp.int32(1), 30 - i)
        cnt = jnp.sum(jnp.where(bits >= cand, 1.0, 0.0), axis=1, keepdims=True)
        return jnp.where(cnt >= cap, cand, prefix)

    thr = lax.fori_loop(0, 31, search, jnp.zeros((ne, 1), I32))
    gt = bits > thr
    eq = bits == thr
    need = cap - jnp.sum(jnp.where(gt, 1.0, 0.0), axis=1, keepdims=True)

    nch = t // LANE
    tri = (lax.broadcasted_iota(I32, (LANE, LANE), 0) <= lax.broadcasted_iota(I32, (LANE, LANE), 1))
    tri = jnp.where(tri, 1.0, 0.0).astype(BF16)

    def chunk(a, c):
        return a[:, c * LANE:(c + 1) * LANE]

    off = jnp.zeros((ne, 1), F32)
    sel = []
    for c in range(nch):
        eqc = jnp.where(chunk(eq, c), 1.0, 0.0)
        inc = jnp.dot(eqc.astype(BF16), tri, preferred_element_type=F32)
        rank = inc - eqc + off
        off = off + inc[:, LANE - 1:LANE]
        sel.append(jnp.where(chunk(gt, c) | (chunk(eq, c) & (rank < need)), 1.0, 0.0))

    lane = lax.broadcasted_iota(I32, (ne, LANE), 1)
    offs = jnp.zeros((ne, LANE), F32)
    off = jnp.zeros((ne, 1), F32)
    per_tile = tt // LANE
    for c in range(nch):
        if c % per_tile == 0:
            offs = jnp.where(lane == c // per_tile, off, offs)
        inc = jnp.dot(sel[c].astype(BF16), tri, preferred_element_type=F32)
        pos = inc - sel[c] + off
        off = off + inc[:, LANE - 1:LANE]
        pos_ref[:, c * LANE:(c + 1) * LANE] = jnp.where(sel[c] > 0.0, pos, -1.0).astype(I32)
    offs = jnp.where(lane == nch // per_tile, off, offs)
    off_ref[...] = offs.astype(I32)


def _select(logits_t, t0_blk, t, cap):
    b, ne, _ = logits_t.shape
    tt = min(TOK_TILE, t)
    kern = functools.partial(_select_kernel, cap=cap, tt=tt)
    return pl.pallas_call(
        kern,
        grid=(b,),
        in_specs=[pl.BlockSpec((None, ne, t), lambda i: (i, 0, t0_blk))],
        out_specs=[pl.BlockSpec((None, ne, t), lambda i: (i, 0, 0)),
                   pl.BlockSpec((None, ne, t), lambda i: (i, 0, 0)),
                   pl.BlockSpec((None, ne, LANE), lambda i: (i, 0, 0))],
        out_shape=[jax.ShapeDtypeStruct((b, ne, t), I32),
                   jax.ShapeDtypeStruct((b, ne, t), F32),
                   jax.ShapeDtypeStruct((b, ne, LANE), I32)],
        compiler_params=_cparams("parallel"),
        name="moe_select",
    )(logits_t)


def _window_onehot(posr, lo, cap, w, rows):
    base = pl.multiple_of(jnp.minimum((lo // WIN_ALIGN) * WIN_ALIGN, cap - w), WIN_ALIGN)
    hit = ((posr - base) == rows) & (posr >= lo)
    return base, hit


def _gather_kernel(off_sm, h_ref, pos_ref, aff_ref, xe_ref, gs_ref, acc, gacc, *, cap, tt, nt, w):
    bi, e = pl.program_id(0), pl.program_id(1)
    ne = pl.num_programs(1)
    acc[...] = jnp.zeros_like(acc)
    gacc[...] = jnp.zeros_like(gacc)
    rows = lax.broadcasted_iota(I32, (w, tt), 0)
    obase = (bi * ne + e) * (nt + 1)

    def tile(j, carry):
        p0, p1 = off_sm[obase + j], off_sm[obase + j + 1]
        posr = pos_ref[pl.ds(j, 1), :]
        affr = aff_ref[pl.ds(j, 1), :]
        ht = h_ref[pl.ds(pl.multiple_of(j * tt, tt), tt), :]

        def window(lo):
            base, hit = _window_onehot(posr, lo, cap, w, rows)
            onehot = jnp.where(hit, 1.0, 0.0).astype(BF16)
            acc[pl.ds(base, w), :] += jnp.dot(onehot, ht, preferred_element_type=F32)
            gacc[pl.ds(base, w), :] += jnp.sum(jnp.where(hit, affr, 0.0), axis=1, keepdims=True)
            return base + w

        nxt = window(p0)

        @pl.when(p1 > nxt)
        def _():
            window(nxt)

        return carry

    lax.fori_loop(0, nt, tile, 0)
    xe_ref[...] = acc[...].astype(BF16)
    gs_ref[...] = gacc[...]


def _gather(off_flat, bm, pos4, aff4, t0_blk, t, cap):
    b, _, d = bm.shape
    ne, nt, tt = pos4.shape[1], pos4.shape[2], pos4.shape[3]
    w = min(tt, cap)
    kern = functools.partial(_gather_kernel, cap=cap, tt=tt, nt=nt, w=w)
    grid_spec = pltpu.PrefetchScalarGridSpec(
        num_scalar_prefetch=1,
        grid=(b, ne),
        in_specs=[pl.BlockSpec((None, t, d), lambda i, e, o: (i, t0_blk, 0)),
                  pl.BlockSpec((None, None, nt, tt), lambda i, e, o: (i, e, 0, 0)),
                  pl.BlockSpec((None, None, nt, tt), lambda i, e, o: (i, e, 0, 0))],
        out_specs=[pl.BlockSpec((None, None, cap, d), lambda i, e, o: (e, i, 0, 0)),
                   pl.BlockSpec((None, None, cap, 1), lambda i, e, o: (e, i, 0, 0))],
        scratch_shapes=[pltpu.VMEM((cap, d), F32), pltpu.VMEM((cap, 1), F32)])
    return pl.pallas_call(
        kern,
        grid_spec=grid_spec,
        out_shape=[jax.ShapeDtypeStruct((ne, b, cap, d), BF16),
                   jax.ShapeDtypeStruct((ne, b, cap, 1), F32)],
        compiler_params=_cparams("parallel", "arbitrary"),
        name="moe_gather",
    )(off_flat, bm, pos4, aff4)


def _ffn_kernel(x_ref, gs_ref, w1_ref, w3_ref, w2_ref, oh_ref, ol_ref, w1b, w3b, w2b):
    @pl.when(pl.program_id(1) == 0)
    def _():
        w1b[...] = w1_ref[...].astype(BF16)
        w3b[...] = w3_ref[...].astype(BF16)
        w2b[...] = w2_ref[...].astype(BF16)

    x = x_ref[...]
    h1 = jnp.dot(x, w1b[...], preferred_element_type=F32)
    h3 = jnp.dot(x, w3b[...], preferred_element_type=F32)
    hid = (h1 * _sigmoid(h1)) * h3
    ye = jnp.dot(hid.astype(BF16), w2b[...], preferred_element_type=F32) * gs_ref[...]
    oh_ref[...], ol_ref[...] = _split2(ye)


def _ffn(xe, gs, w1, w3, w2, layer):
    ne, r, d = xe.shape
    f = w1.shape[-1]
    tr = min(r, 512)
    return pl.pallas_call(
        _ffn_kernel,
        grid=(ne, r // tr),
        in_specs=[pl.BlockSpec((None, tr, d), lambda e, i: (e, i, 0)),
                  pl.BlockSpec((None, tr, 1), lambda e, i: (e, i, 0)),
                  pl.BlockSpec((None, None, d, f), lambda e, i: (layer, e, 0, 0)),
                  pl.BlockSpec((None, None, d, f), lambda e, i: (layer, e, 0, 0)),
                  pl.BlockSpec((None, None, f, d), lambda e, i: (layer, e, 0, 0))],
        out_specs=[pl.BlockSpec((None, tr, d), lambda e, i: (e, i, 0))] * 2,
        out_shape=[jax.ShapeDtypeStruct((ne, r, d), BF16)] * 2,
        scratch_shapes=[pltpu.VMEM((d, f), BF16), pltpu.VMEM((d, f), BF16), pltpu.VMEM((f, d), BF16)],
        compiler_params=_cparams("parallel", "arbitrary"),
        name="moe_ffn",
    )(xe, gs, w1, w3, w2)


def _combine_kernel(off_sm, x1_ref, mod_ref, pos_ref, yh_ref, yl_ref, o_ref, *, cap, tt, nt, w, tiles_per_blk):
    bi, ci, e = pl.program_id(0), pl.program_id(1), pl.program_id(2)
    ne = pl.num_programs(2)

    @pl.when(e == 0)
    def _():
        o_ref[...] = jnp.zeros_like(o_ref)

    rows = lax.broadcasted_iota(I32, (w, tt), 0)
    obase = (bi * ne + e) * (nt + 1)

    def tile(jj, carry):
        j = ci * tiles_per_blk + jj
        p0, p1 = off_sm[obase + j], off_sm[obase + j + 1]
        posr = pos_ref[pl.ds(j, 1), :]
        r0 = pl.multiple_of(jj * tt, tt)

        def window(lo):
            base, hit = _window_onehot(posr, lo, cap, w, rows)
            onehot = jnp.where(hit, 1.0, 0.0).astype(BF16)
            d = functools.partial(lax.dot_general, dimension_numbers=_TN, preferred_element_type=F32)
            o_ref[pl.ds(r0, tt), :] += d(onehot, yh_ref[pl.ds(base, w), :]) + d(onehot, yl_ref[pl.ds(base, w), :])
            return base + w

        nxt = window(p0)

        @pl.when(p1 > nxt)
        def _():
            window(nxt)

        return carry

    lax.fori_loop(0, tiles_per_blk, tile, 0)

    @pl.when(e == ne - 1)
    def _():
        o_ref[...] = x1_ref[...] + mod_ref[...][5:6] * o_ref[...]


def _combine(off_flat, x1, mod, stream, pos4, ye, t0_blk, t, cap):
    b, s, d = x1.shape
    ne, nt, tt = pos4.shape[1], pos4.shape[2], pos4.shape[3]
    w = min(tt, cap)
    tc = min(t, 2048)
    nblk = t // tc
    kern = functools.partial(_combine_kernel, cap=cap, tt=tt, nt=nt, w=w, tiles_per_blk=tc // tt)
    grid_spec = pltpu.PrefetchScalarGridSpec(
        num_scalar_prefetch=1,
        grid=(b, nblk, ne),
        in_specs=[pl.BlockSpec((None, tc, d), lambda i, c, e, o: (i, t0_blk * nblk + c, 0)),
                  pl.BlockSpec((None, None, 6, d), lambda i, c, e, o: (i, stream, 0, 0)),
                  pl.BlockSpec((None, None, nt, tt), lambda i, c, e, o: (i, e, 0, 0)),
                  pl.BlockSpec((None, None, cap, d), lambda i, c, e, o: (e, i, 0, 0)),
                  pl.BlockSpec((None, None, cap, d), lambda i, c, e, o: (e, i, 0, 0))],
        out_specs=pl.BlockSpec((None, tc, d), lambda i, c, e, o: (i, t0_blk * nblk + c, 0)))
    return pl.pallas_call(
        kern,
        grid_spec=grid_spec,
        out_shape=jax.ShapeDtypeStruct((b, s, d), F32),
        input_output_aliases={1: 0},
        compiler_params=_cparams("parallel", "parallel", "arbitrary"),
        name="moe_combine",
    )(off_flat, x1, mod, pos4, *ye)


def _moe_stream(x1, bm, logits_t, mod, stream, t0_blk, t, experts):
    b = x1.shape[0]
    cap = CAPACITY_FACTOR * t // N_EXPERTS
    pos, aff, offs = _select(logits_t, t0_blk, t, cap)
    tt = min(TOK_TILE, t)
    nt = t // tt
    pos4 = pos.reshape(b, N_EXPERTS, nt, tt)
    aff4 = aff.reshape(b, N_EXPERTS, nt, tt)
    off_flat = offs[:, :, :nt + 1].reshape(-1)
    xe, gs = _gather(off_flat, bm, pos4, aff4, t0_blk, t, cap)
    d = x1.shape[-1]
    ye = _ffn(xe.reshape(N_EXPERTS, b * cap, d), gs.reshape(N_EXPERTS, b * cap, 1), *experts)
    ye = [y.reshape(N_EXPERTS, b, cap, d) for y in ye]
    return _combine(off_flat, x1, mod, stream, pos4, ye, t0_blk, t, cap)


def _rwkv_stream_kernel(*refs, t_lat, has_vres):
    if has_vres:
        (x_ref, xp_ref, xn_ref, mod_ref, g1_ref, mu_ref, wr_ref, wk_ref, wv_ref, w1_ref, w2_ref, w0_ref,
         a1_ref, a2_ref, a0_ref, gg1_ref, gg2_ref, kk_ref, ka_ref, rk_ref, seg_ref, segt_ref,
         vf_ref, v0_ref, v1_ref, v2_ref,
         r_out, v_out, kk_out, lw_out, kd_out, b_out, g_out, bonus_out) = refs
    else:
        (x_ref, xp_ref, xn_ref, mod_ref, g1_ref, mu_ref, wr_ref, wk_ref, wv_ref, w1_ref, w2_ref, w0_ref,
         a1_ref, a2_ref, a0_ref, gg1_ref, gg2_ref, kk_ref, ka_ref, rk_ref, seg_ref, segt_ref,
         r_out, v_out, kk_out, lw_out, kd_out, b_out, g_out, bonus_out) = refs
    j = pl.program_id(1)
    mod = mod_ref[...]
    g1 = g1_ref[...]
    nm = lambda x: _norm_mod(x, g1, mod[1:2], mod[0:1])
    h = nm(x_ref[...])
    tm, d = h.shape
    hp = nm(xp_ref[...])[SUBLANE - 1:SUBLANE]
    hn = nm(xn_ref[...])[0:1]
    row = lax.broadcasted_iota(I32, (tm, 1), 0)
    grow = row + j * tm
    s_tot = pl.num_programs(1) * tm
    first = (grow == 0) | (grow == t_lat)
    last = (grow == t_lat - 1) | (grow == s_tot - 1)
    prev = jnp.where(row == 0, hp, pltpu.roll(h, 1, 0))
    prev = jnp.where(first, 0.0, prev)
    nxt = jnp.where(row == tm - 1, hn, pltpu.roll(h, tm - 1, 0))
    nxt = jnp.where(last, 0.0, nxt)
    xx = 0.5 * (prev + nxt) - h
    mu = mu_ref[...]
    xs = [h + xx * mu[m:m + 1] for m in range(6)]
    r = _bdot(xs[0], wr_ref[...])
    k = _bdot(xs[1], wk_ref[...])
    v = _bdot(xs[2], wv_ref[...])
    if has_vres:
        gate = _sigmoid(v0_ref[...] + _bdot(_bdot(xs[2], v1_ref[...]), v2_ref[...]))
        v = v + (vf_ref[...] - v) * gate
    wz = w0_ref[...] + _bdot(jnp.tanh(_bdot(xs[3], w1_ref[...])), w2_ref[...])
    lw = -math.exp(-0.5) * _sigmoid(wz)
    a = _sigmoid(a0_ref[...] + _bdot(_bdot(xs[4], a1_ref[...]), a2_ref[...]))
    g = _bdot(_sigmoid(_bdot(xs[5], gg1_ref[...])), gg2_ref[...])
    seg, segt = seg_ref[...], segt_ref[...]
    segsum = lambda t: _dot_exact_rhs01(_dot_exact_rhs01(t, seg), segt)
    kkr = k * kk_ref[...]
    kk = kkr / jnp.maximum(jnp.sqrt(segsum(kkr * kkr)), 1e-12)
    ka = ka_ref[...]
    kd0 = k * (1.0 + (a[:, :d] - 1.0) * ka)
    kd1 = k * (1.0 + (a[:, d:] - 1.0) * ka)
    r_out[...] = r
    v_out[...] = v
    kk_out[...] = kk
    lw_out[...] = lw
    kd_out[:, :d] = kd0
    kd_out[:, d:] = kd1
    b_out[:, :d] = kk * a[:, :d]
    b_out[:, d:] = kk * a[:, d:]
    g_out[...] = g
    bonus_out[...] = segsum(r * (kd0 + kd1) * rk_ref[...]) * v


def _rwkv_stream(x, mod, g1, wts, vres, t_lat):
    b, s, d = x.shape
    tm = TOK_TILE
    nlt = t_lat // tm
    nsub = tm // SUBLANE
    const = lambda a: pl.BlockSpec(a.shape, lambda i, j: (0,) * a.ndim)
    tok = lambda w: pl.BlockSpec((None, tm, w), lambda i, j: (i, j, 0))
    in_specs = [tok(d),
                pl.BlockSpec((None, SUBLANE, d), lambda i, j: (i, jnp.maximum(j * nsub - 1, 0), 0)),
                pl.BlockSpec((None, SUBLANE, d), lambda i, j: (i, jnp.minimum((j + 1) * nsub, s // SUBLANE - 1), 0)),
                pl.BlockSpec((None, None, 6, d), lambda i, j: (i, (j >= nlt).astype(I32), 0, 0)),
                const(g1)] + [const(a) for a in wts]
    args = [x, x, x, mod, g1] + list(wts)
    if vres is not None:
        vf, v0, v1, v2 = vres
        in_specs += [tok(d), const(v0), const(v1), const(v2)]
        args += [vf, v0, v1, v2]
    kern = functools.partial(_rwkv_stream_kernel, t_lat=t_lat, has_vres=vres is not None)
    widths = [d, d, d, 2 * d, 2 * d, 2 * d, d, d]
    return pl.pallas_call(
        kern,
        grid=(b, s // tm),
        in_specs=in_specs,
        out_specs=[tok(w) for w in widths],
        out_shape=[jax.ShapeDtypeStruct((b, s, w), F32) for w in widths],
        compiler_params=_cparams("parallel", "parallel"),
        name="rwkv_stream",
    )(*args)


def _scan_kernel(r_ref, v_ref, kk_ref, lw_ref, kd_ref, b_ref, y_ref, h_ref, *, reverse, nchunk):
    L = SCAN_CHUNK

    @pl.when(pl.program_id(2) == 0)
    def _():
        h_ref[...] = jnp.zeros_like(h_ref)

    n2 = 2 * L
    rr = lax.broadcasted_iota(I32, (n2, n2), 0)
    cc = lax.broadcasted_iota(I32, (n2, n2), 1)
    tr, tc = rr % L, cc % L
    same = (rr // L) == (cc // L)
    before = (tr < tc) if reverse else (tr > tc)
    strict = same & before
    incl = same & (before | (tr == tc))
    eye = rr == cc
    blk16 = (rr // 16) == (cc // 16)
    blk32 = (rr // 32) == (cc // 32)
    ri = lax.broadcasted_iota(I32, (L, L), 0)
    ci = lax.broadcasted_iota(I32, (L, L), 1)
    tri = jnp.where((ci >= ri) if reverse else (ci <= ri), 1.0, 0.0).astype(BF16)
    head0 = lax.broadcasted_iota(I32, (L, LANE), 1) < RW_HEAD

    def stack(x):
        return jnp.concatenate([jnp.where(head0, x, 0.0), jnp.where(head0, 0.0, x)], axis=0)

    order = list(range(nchunk - 1, -1, -1) if reverse else range(nchunk))
    npp = y_ref.shape[-1] // LANE
    units = [(pp, ch) for pp in range(npp) for ch in order]

    def each(f, *lists):
        return [f(*a) for a in zip(*lists)]

    def ld(ref):
        return [ref[pl.ds(ch * L, L), pl.ds(pp * LANE, LANE)] for pp, ch in units]

    dd = functools.partial(jnp.dot, preferred_element_type=F32)
    lw = ld(lw_ref)
    lws = each(_split3, lw)
    cs = each(lambda t: dd(tri, t[0]) + (dd(tri, t[1]) + dd(tri, t[2])), lws)
    total = each(lambda c: c[0:1] if reverse else c[L - 1:L], cs)
    gam = each(jnp.exp, cs)
    ginv = each(lambda c: jnp.exp(-c), cs)
    gprev = each(lambda c, l: jnp.exp(c - l), cs, lw)
    gend = each(lambda t, c: jnp.exp(t - c), total, cs)
    kk, bb, kd, r, v = ld(kk_ref), ld(b_ref), ld(kd_ref), ld(r_ref), ld(v_ref)
    mul_stack = lambda a, g: stack(a * g)
    rtm = each(mul_stack, r, gam)
    rts = each(_split2, rtm)
    kkm, btm, ktm = (each(_split2, each(mul_stack, a, g)) for a, g in ((kk, gprev), (bb, ginv), (kd, ginv)))
    bht, kht = (each(lambda a, g: _split2(stack(a * g).T), a, gend) for a in (bb, kd))
    vm = each(_split2, each(stack, v))
    n_ab = each(lambda a, c, e: _mm3_rows([a, c], e, _NT), kkm, rts, btm)
    m_ak = each(lambda a, c, e: _mm3_rows([a, c], e, _NT), kkm, rts, ktm)
    n = each(lambda t: jnp.where(strict, t[0], 0.0), n_ab)
    ab = each(lambda t: jnp.where(incl, t[1], 0.0), n_ab)
    m = each(lambda t: jnp.where(strict, t[0], 0.0), m_ak)
    ak = each(lambda t: jnp.where(incl, t[1], 0.0), m_ak)
    nd = each(lambda t: jnp.where(blk16, t, 0.0), n)
    x = each(lambda t: jnp.where(eye, 1.0, 0.0) - t, nd)
    pw = nd
    for _ in range(3):
        pw = each(lambda t: _bdot(t, t), pw)
        x = each(lambda a, c: a + _bdot(a, c), x, pw)
    ident = jnp.where(eye, 1.0, 0.0)
    for inner, outer in ((blk16, blk32), (blk32, same)):
        c = each(lambda t: jnp.where(outer & jnp.logical_not(inner), t, 0.0), n)
        e = each(lambda a: a - ident, x)
        ec = each(_bdot, e, c)
        x = each(lambda a, t, u, s: a - t - (u + _bdot(t + u, s)), x, c, ec, e)
    xs = each(_split2, x)
    abs_ = each(_split2, ab)
    mv_akv_khv = each(lambda a, c, e, f: _mm3_rows([_split2(a), _split2(c), e], f), m, ak, kht, vm)
    cat = lambda a, c: tuple(jnp.concatenate([s, t], axis=1) for s, t in zip(a, c))
    wu = each(lambda a, c, t: _split2(_mm3(a, cat(c, _split2(t[0])))), xs, kkm, mv_akv_khv)
    bw_abw = each(lambda a, c, e: _mm3_rows([a, c], e), bht, abs_, wu)
    p = each(lambda t, a: jnp.where(eye, jnp.exp(t), 0.0) - a[0][:, :LANE], total, bw_abw)
    q = each(lambda t, a: t[2] - a[0][:, LANE:], mv_akv_khv, bw_abw)
    rres = each(lambda t, a: t - a[1][:, :LANE], rtm, bw_abw)
    y0 = each(lambda t, a: t[1] - a[1][:, LANE:], mv_akv_khv, bw_abw)
    for i, (pp, ch) in enumerate(units):
        y, hn = _mm3_rows([_split2(rres[i]), _split2(p[i])], _split2(h_ref[pp]))
        y = y + y0[i]
        y_ref[pl.ds(ch * L, L), pl.ds(pp * LANE, LANE)] = y[:L] + y[L:]
        h_ref[pp] = hn + q[i]


def _scan(r, v, kk, lw, kd, bb, t_lat, reverse):
    b, s, d = r.shape
    tb = TOK_TILE
    nl, nc = t_lat // tb, (s - t_lat) // tb
    z = 1 if reverse else 0

    def blk(c):
        if reverse:
            return jnp.where(c < nc, nl + nc - 1 - c, nl - 1 - (c - nc))
        return jnp.where(c < nc, nl + c, c - nc)

    wl = SCAN_PAIRS * LANE
    ngrp = d // wl
    shared = pl.BlockSpec((None, tb, wl), lambda i, p, c: (i, blk(c), p))
    dirn = pl.BlockSpec((None, tb, wl), lambda i, p, c: (i, blk(c), z * ngrp + p))
    kern = functools.partial(_scan_kernel, reverse=reverse, nchunk=tb // SCAN_CHUNK)
    return pl.pallas_call(
        kern,
        grid=(b, ngrp, nl + nc),
        in_specs=[shared, shared, shared, dirn, dirn, dirn],
        out_specs=pl.BlockSpec((None, tb, wl), lambda i, p, c: (i, blk(c), p)),
        out_shape=jax.ShapeDtypeStruct((b, s, d), F32),
        scratch_shapes=[pltpu.VMEM((SCAN_PAIRS, LANE, LANE), F32)],
        compiler_params=_cparams("parallel", "parallel", "arbitrary"),
        name="wkv_scan_bwd" if reverse else "wkv_scan_fwd",
    )(r, v, kk, lw, kd, bb)


def _rwkv_out_kernel(yf_ref, yb_ref, bonus_ref, g_ref, lg_ref, lb_ref, seg_ref, segt_ref, o_ref):
    seg, segt = seg_ref[...], segt_ref[...]
    segmean = lambda t: _dot_exact_rhs01(_dot_exact_rhs01(t, seg), segt) * (1.0 / RW_HEAD)
    y = yf_ref[...] + yb_ref[...]
    dlt = y - segmean(y)
    var = segmean(dlt * dlt)
    yn = (dlt * lax.rsqrt(var + RW_LNX_EPS)) * lg_ref[...] + lb_ref[...]
    o_ref[...] = ((yn + bonus_ref[...]) * g_ref[...]).astype(BF16)


def _rwkv_out(yf, yb, bonus, g, lnx_g, lnx_b, seg, segt):
    b, s, d = yf.shape
    tm = TOK_TILE
    tok = pl.BlockSpec((None, tm, d), lambda i, j: (i, j, 0))
    const = lambda a: pl.BlockSpec(a.shape, lambda i, j: (0,) * a.ndim)
    return pl.pallas_call(
        _rwkv_out_kernel,
        grid=(b, s // tm),
        in_specs=[tok, tok, tok, tok, const(lnx_g), const(lnx_b), const(seg), const(segt)],
        out_specs=tok,
        out_shape=jax.ShapeDtypeStruct((b, s, d), BF16),
        compiler_params=_cparams("parallel", "parallel"),
        name="rwkv_out",
    )(yf, yb, bonus, g, lnx_g, lnx_b, seg, segt)


def _final_kernel(x_ref, g_ref, o_ref):
    o_ref[...] = _rms(x_ref[...], g_ref[...])


def _final_norm(x, g, t_lat):
    b, _, d = x.shape
    tm = TOK_TILE
    return pl.pallas_call(
        _final_kernel,
        grid=(b, t_lat // tm),
        in_specs=[pl.BlockSpec((None, tm, d), lambda i, j: (i, j, 0)),
                  pl.BlockSpec(g.shape, lambda i, j: (0, 0))],
        out_specs=pl.BlockSpec((None, tm, d), lambda i, j: (i, j, 0)),
        out_shape=jax.ShapeDtypeStruct((b, t_lat, d), F32),
        compiler_params=_cparams("parallel", "parallel"),
        name="final_norm",
    )(x, g)


def _rope_tables(t_lat, t_ctx):
    rows = t_lat // GRID_W
    r = jnp.broadcast_to(jnp.arange(rows)[:, None], (rows, GRID_W)).reshape(-1).astype(F32)
    col = jnp.broadcast_to(jnp.arange(GRID_W)[None, :], (rows, GRID_W)).reshape(-1).astype(F32)
    n_freq = MLA_ROPE // 4
    inv = ROPE_THETA ** (-jnp.arange(n_freq, dtype=F32) / n_freq)
    ang = jnp.concatenate([r[:, None] * inv, col[:, None] * inv], axis=-1)
    cos, sin = jnp.cos(ang), jnp.sin(ang)
    cosf = jnp.concatenate([cos, cos, cos, cos], axis=-1)
    sins = jnp.concatenate([-sin, sin, -sin, sin], axis=-1)
    cosf = jnp.concatenate([cosf, jnp.ones((t_ctx, LANE), F32)], axis=0)
    sins = jnp.concatenate([sins, jnp.zeros((t_ctx, LANE), F32)], axis=0)
    return cosf, sins


def _blockdiag2(w):
    z = jnp.zeros_like(w[0])
    return jnp.concatenate([jnp.concatenate([w[0], z], axis=1), jnp.concatenate([z, w[1]], axis=1)], axis=0)


def kernel(x, c, ctx, c_ctx, mod_w, mod_b, norm1_g, norm2_g, final_g, mla_w_in, mla_g_q, mla_g_kv, mla_w_uq, mla_w_ukv, mla_w_o, rw_mu, rw_w_rkv, rw_w0, rw_w1, rw_w2, rw_a0, rw_a1, rw_a2, rw_v0, rw_v1, rw_v2, rw_g1, rw_g2, rw_k_k, rw_k_a, rw_r_k, rw_lnx_g, rw_lnx_b, rw_w_o, moe_router, moe_w1, moe_w3, moe_w2):
    b, t_lat, d = x.shape
    t_ctx = ctx.shape[1]
    depth = mod_w.shape[0]
    assert b + 1 <= SUBLANE and t_lat % TOK_TILE == 0 and t_ctx % TOK_TILE == 0 and d % LANE == 0
    n_lat_tiles = t_lat // TOK_TILE
    ctx_blk = t_lat // t_ctx
    assert ctx_blk * t_ctx == t_lat

    cc = jnp.concatenate([c, c_ctx[None], jnp.zeros((SUBLANE - b - 1, d), F32)], axis=0)
    mods = _modulation(cc, mod_w, mod_b).reshape(depth, SUBLANE, 6, d)
    mod_all = jnp.stack([mods[:, :b], jnp.broadcast_to(mods[:, b:b + 1], (depth, b, 6, d))], axis=2)

    xs = jnp.concatenate([x, ctx], axis=1)
    cosf, sins = _rope_tables(t_lat, t_ctx)
    row = lambda a: a.reshape(1, -1)
    seg = (jnp.arange(d)[:, None] // RW_HEAD == jnp.arange(LANE)[None, :]).astype(BF16)
    segt = seg.T

    v_first = None
    for layer in range(depth):
        need_ctx = layer < depth - 1
        mod = mod_all[layer]
        j = layer // 2
        if layer % 2 == 0:
            win = jnp.pad(mla_w_in[j], ((0, 0), (0, 4 * LANE - mla_w_in.shape[-1]))).astype(BF16)
            wuq = mla_w_uq[j].reshape(MLA_Q_LORA, MLA_HEADS, MLA_NOPE + MLA_ROPE)
            wuq = jnp.concatenate([wuq[..., :MLA_NOPE].reshape(MLA_Q_LORA, -1),
                                   wuq[..., MLA_NOPE:].reshape(MLA_Q_LORA, -1)], axis=1).astype(BF16)
            q, k, v = _mla_proj(xs, mod, row(norm1_g[layer]), win, row(mla_g_q[j]), row(mla_g_kv[j]),
                                wuq, mla_w_ukv[j].astype(BF16), cosf, sins, n_lat_tiles)
            y = _attention(q, k, v, t_lat)
            wo = mla_w_o[j].astype(BF16)
        else:
            lora = lambda w: jnp.concatenate([w[0], w[1]], axis=1).astype(BF16)
            wts = [rw_mu[j], rw_w_rkv[j, 0].astype(BF16), rw_w_rkv[j, 1].astype(BF16), rw_w_rkv[j, 2].astype(BF16),
                   lora(rw_w1[j]), _blockdiag2(rw_w2[j]).astype(BF16), rw_w0[j].reshape(1, -1),
                   lora(rw_a1[j]), _blockdiag2(rw_a2[j]).astype(BF16), rw_a0[j].reshape(1, -1),
                   rw_g1[j].astype(BF16), rw_g2[j].astype(BF16),
                   row(rw_k_k[j]), row(rw_k_a[j]), row(rw_r_k[j]), seg, segt]
            vres = None
            if j > 0:
                pad = LANE - rw_v1.shape[-1]
                vres = (v_first, row(rw_v0[j - 1]), jnp.pad(rw_v1[j - 1], ((0, 0), (0, pad))).astype(BF16),
                        jnp.pad(rw_v2[j - 1], ((0, pad), (0, 0))).astype(BF16))
            r, v, kk, lw, kd, bb, g, bonus = _rwkv_stream(xs, mod, row(norm1_g[layer]), wts, vres, t_lat)
            if j == 0:
                v_first = v
            yf = _scan(r, v, kk, lw, kd, bb, t_lat, reverse=False)
            yb = _scan(r, v, kk, lw, kd, bb, t_lat, reverse=True)
            y = _rwkv_out(yf, yb, bonus, g, row(rw_lnx_g[j]), row(rw_lnx_b[j]), seg, segt)
            wo = rw_w_o[j].astype(BF16)
        x1, bm, logits_t = _post_mixer(xs, y, wo, mod, row(norm2_g[layer]), moe_router[layer].T.astype(BF16),
                                       n_lat_tiles)
        experts = (moe_w1, moe_w3, moe_w2, layer)
        xs = _moe_stream(x1, bm, logits_t, mod, 0, 0, t_lat, experts)
        if need_ctx:
            xs = _moe_stream(xs, bm, logits_t, mod, 1, ctx_blk, t_ctx, experts)
    return _final_norm(xs, row(final_g), t_lat)
```

```python
import functools
import math

import jax
import jax.numpy as jnp
from jax import lax
from jax.experimental import pallas as pl
from jax.experimental.pallas import tpu as pltpu

F32, BF16, I32 = jnp.float32, jnp.bfloat16, jnp.int32

GRID_W = 64
MLA_HEADS, MLA_NOPE, MLA_ROPE, MLA_V = 8, 128, 64, 128
MLA_Q_LORA, MLA_KV_LORA = 256, 128
MLA_SCALE = (MLA_NOPE + MLA_ROPE) ** -0.5
ROPE_THETA = 10000.0
RW_HEAD = 64
RW_LNX_EPS = 64e-5
N_EXPERTS = 16
CAPACITY_FACTOR = 2
RMS_EPS = 1e-6

LANE = 128
SUBLANE = 8
TOK_TILE = 256
WIN_ALIGN = 16
SLOT_WINDOW = 64
COMBINE_SPLIT = 2
ATTN_TILE = 1024
ATTN_ROWS = 256
SCAN_CHUNK = 64
SCAN_PAIRS = 4
VMEM_LIMIT = 56 * 1024 * 1024


def _cparams(*sem):
    return pltpu.CompilerParams(dimension_semantics=sem, vmem_limit_bytes=VMEM_LIMIT)


def _bdot(a, b):
    return jnp.dot(a.astype(BF16), b.astype(BF16), preferred_element_type=F32)


_NT = (((1,), (1,)), ((), ()))
_TN = (((0,), (0,)), ((), ()))
_NN = (((1,), (0,)), ((), ()))


def _split2(x):
    hi = x.astype(BF16)
    lo = (x - hi.astype(F32)).astype(BF16)
    return hi, lo


def _split3(x):
    hi = x.astype(BF16)
    r = x - hi.astype(F32)
    mid = r.astype(BF16)
    lo = (r - mid.astype(F32)).astype(BF16)
    return hi, mid, lo


def _mm3(a, b, dims=_NN):
    d = functools.partial(lax.dot_general, dimension_numbers=dims, preferred_element_type=F32)
    return d(a[0], b[0]) + (d(a[0], b[1]) + d(a[1], b[0]))


def _mm3_rows(lhs, b, dims=_NN):
    his = jnp.concatenate([a[0] for a in lhs], axis=0)
    los = jnp.concatenate([a[1] for a in lhs], axis=0)
    d = functools.partial(lax.dot_general, dimension_numbers=dims, preferred_element_type=F32)
    nr = his.shape[0]
    big = d(jnp.concatenate([his, los], axis=0), b[0])
    res = big[:nr] + (d(his, b[1]) + big[nr:])
    outs, o = [], 0
    for a in lhs:
        outs.append(res[o:o + a[0].shape[0]])
        o += a[0].shape[0]
    return outs


def _dot_exact_rhs01(x, m01):
    h, m, l = _split3(x)
    d = functools.partial(jnp.dot, preferred_element_type=F32)
    return d(h, m01) + (d(m, m01) + d(l, m01))


def _rms(x, g):
    return (x * lax.rsqrt(jnp.mean(x * x, axis=-1, keepdims=True) + RMS_EPS)) * g


def _norm_mod(x, g, scale, shift):
    return _rms(x, g) * (1.0 + scale) + shift


def _sigmoid(x):
    return 1.0 / (1.0 + jnp.exp(-x))


def _mod_kernel(c_ref, w_ref, b_ref, o_ref):
    s = c_ref[...]
    s = s * _sigmoid(s)
    o_ref[...] = _bdot(s, w_ref[...]) + b_ref[...]


def _modulation(cc, mod_w, mod_b):
    depth, d, n = mod_w.shape
    tn = n // 4
    return pl.pallas_call(
        _mod_kernel,
        grid=(depth, n // tn),
        in_specs=[pl.BlockSpec((SUBLANE, d), lambda l, j: (0, 0)),
                  pl.BlockSpec((None, d, tn), lambda l, j: (l, 0, j)),
                  pl.BlockSpec((None, 1, tn), lambda l, j: (l, 0, j))],
        out_specs=pl.BlockSpec((None, SUBLANE, tn), lambda l, j: (l, 0, j)),
        out_shape=jax.ShapeDtypeStruct((depth, SUBLANE, n), F32),
        compiler_params=_cparams("arbitrary", "arbitrary"),
        name="modulation",
    )(cc, mod_w, mod_b.reshape(depth, 1, n))


def _rope128(x, cosf, sins):
    lane = lax.broadcasted_iota(I32, x.shape, 1)
    first = (lane % MLA_ROPE) < (MLA_ROPE // 2)
    rot = jnp.where(first, pltpu.roll(x, LANE - MLA_ROPE // 2, 1), pltpu.roll(x, MLA_ROPE // 2, 1))
    return x * cosf + rot * sins


def _mla_proj_kernel(x_ref, mod_ref, g1_ref, win_ref, gq_ref, gkv_ref, wuq_ref, wukv_ref,
                     cos_ref, sin_ref, q_ref, k_ref, v_ref):
    mod = mod_ref[...]
    h = _norm_mod(x_ref[...], g1_ref[...], mod[1:2], mod[0:1])
    z = _bdot(h, win_ref[...])
    cq = _rms(z[:, :MLA_Q_LORA], gq_ref[...])
    ckv = _rms(z[:, MLA_Q_LORA:MLA_Q_LORA + MLA_KV_LORA], gkv_ref[...])
    kr = z[:, MLA_Q_LORA + MLA_KV_LORA:]
    q = _bdot(cq, wuq_ref[...])
    kv = _bdot(ckv, wukv_ref[...])
    cosf, sins = cos_ref[...], sin_ref[...]
    kr_lo = _rope128(kr, cosf, sins)
    kr_hi = pltpu.roll(kr_lo, MLA_ROPE, 1)
    nope_w = MLA_HEADS * MLA_NOPE
    for g in range(MLA_HEADS // 2):
        qr = _rope128(q[:, nope_w + LANE * g:nope_w + LANE * (g + 1)], cosf, sins)
        for hh in (2 * g, 2 * g + 1):
            qn = q[:, MLA_NOPE * hh:MLA_NOPE * (hh + 1)]
            q_ref[hh] = jnp.concatenate([qn, qr], axis=1).astype(BF16)
    for hh in range(MLA_HEADS):
        base = (MLA_NOPE + MLA_V) * hh
        krh = kr_lo if hh % 2 == 0 else kr_hi
        k_ref[hh] = jnp.concatenate([kv[:, base:base + MLA_NOPE], krh], axis=1).astype(BF16)
        v_ref[hh] = kv[:, base + MLA_NOPE:base + MLA_NOPE + MLA_V].astype(BF16)


def _mla_proj(x, mod, g1, win, gq, gkv, wuq, wukv, cosf, sins, n_lat_tiles):
    b, s, d = x.shape
    tm = TOK_TILE
    const = lambda shape: pl.BlockSpec(shape, lambda i, j: (0,) * len(shape))
    hspec = lambda w: pl.BlockSpec((None, MLA_HEADS, tm, w), lambda i, j: (i, 0, j, 0))
    return pl.pallas_call(
        _mla_proj_kernel,
        grid=(b, s // tm),
        in_specs=[pl.BlockSpec((None, tm, d), lambda i, j: (i, j, 0)),
                  pl.BlockSpec((None, None, 6, d), lambda i, j: (i, (j >= n_lat_tiles).astype(I32), 0, 0)),
                  const(g1.shape), const(win.shape), const(gq.shape), const(gkv.shape),
                  const(wuq.shape), const(wukv.shape),
                  pl.BlockSpec((tm, LANE), lambda i, j: (j, 0)),
                  pl.BlockSpec((tm, LANE), lambda i, j: (j, 0))],
        out_specs=[hspec(2 * LANE), hspec(2 * LANE), hspec(MLA_V)],
        out_shape=[jax.ShapeDtypeStruct((b, MLA_HEADS, s, 2 * LANE), BF16),
                   jax.ShapeDtypeStruct((b, MLA_HEADS, s, 2 * LANE), BF16),
                   jax.ShapeDtypeStruct((b, MLA_HEADS, s, MLA_V), BF16)],
        compiler_params=_cparams("parallel", "parallel"),
        name="mla_proj",
    )(x, mod, g1, win, gq, gkv, wuq, wukv, cosf, sins)


def _attn_kernel(q_ref, k_ref, v_ref, o_ref, *, n_lat_tiles, t_lat):
    j = pl.program_id(2)
    t_ctx = k_ref.shape[0] - t_lat

    def softmax_pv(s, v):
        m = jnp.max(s, axis=-1, keepdims=True)
        p = jnp.exp2((s - m) * (MLA_SCALE * math.log2(math.e)))
        l = jnp.sum(p, axis=-1, keepdims=True)
        o = jnp.dot(p.astype(BF16), v, preferred_element_type=F32)
        return (o / l).astype(o_ref.dtype)

    @pl.when(j < n_lat_tiles)
    def _():
        for r0 in range(0, o_ref.shape[0], ATTN_ROWS):
            s = lax.dot_general(q_ref[r0:r0 + ATTN_ROWS, :], k_ref[...], _NT, preferred_element_type=F32)
            o_ref[r0:r0 + ATTN_ROWS, :] = softmax_pv(s, v_ref[...])

    @pl.when(j >= n_lat_tiles)
    def _():
        s = lax.dot_general(q_ref[:t_ctx, :], k_ref[t_lat:, :], _NT, preferred_element_type=F32)
        o_ref[:t_ctx, :] = softmax_pv(s, v_ref[t_lat:, :])


def _attention(q, k, v, t_lat):
    b, nh, s, _ = q.shape
    tq = min(ATTN_TILE, t_lat)
    assert t_lat % tq == 0 and s - t_lat <= tq
    kern = functools.partial(_attn_kernel, n_lat_tiles=t_lat // tq, t_lat=t_lat)
    return pl.pallas_call(
        kern,
        grid=(b, nh, pl.cdiv(s, tq)),
        in_specs=[pl.BlockSpec((None, None, tq, q.shape[-1]), lambda i, h, j: (i, h, j, 0)),
                  pl.BlockSpec((None, None, s, k.shape[-1]), lambda i, h, j: (i, h, 0, 0)),
                  pl.BlockSpec((None, None, s, v.shape[-1]), lambda i, h, j: (i, h, 0, 0))],
        out_specs=pl.BlockSpec((None, tq, MLA_V), lambda i, h, j: (i, j, h)),
        out_shape=jax.ShapeDtypeStruct((b, s, nh * MLA_V), BF16),
        compiler_params=_cparams("parallel", "parallel", "arbitrary"),
        name="mla_attention",
    )(q, k, v)


def _post_mixer_kernel(x_ref, y_ref, wo_ref, mod_ref, g2_ref, rt_ref, x1_ref, bm_ref, lg_ref):
    mod = mod_ref[...]
    x1 = x_ref[...] + mod[2:3] * jnp.dot(y_ref[...], wo_ref[...], preferred_element_type=F32)
    x1_ref[...] = x1
    bm = _norm_mod(x1, g2_ref[...], mod[4:5], mod[3:4]).astype(BF16)
    bm_ref[...] = bm
    lg_ref[...] = lax.dot_general(rt_ref[...], bm, _NT, preferred_element_type=F32)


def _post_mixer(x, y, wo, mod, g2, router_t, n_lat_tiles):
    b, s, d = x.shape
    tm = TOK_TILE
    const = lambda shape: pl.BlockSpec(shape, lambda i, j: (0,) * len(shape))
    return pl.pallas_call(
        _post_mixer_kernel,
        grid=(b, s // tm),
        in_specs=[pl.BlockSpec((None, tm, d), lambda i, j: (i, j, 0)),
                  pl.BlockSpec((None, tm, y.shape[-1]), lambda i, j: (i, j, 0)),
                  const(wo.shape),
                  pl.BlockSpec((None, None, 6, d), lambda i, j: (i, (j >= n_lat_tiles).astype(I32), 0, 0)),
                  const(g2.shape), const(router_t.shape)],
        out_specs=[pl.BlockSpec((None, tm, d), lambda i, j: (i, j, 0)),
                   pl.BlockSpec((None, tm, d), lambda i, j: (i, j, 0)),
                   pl.BlockSpec((None, N_EXPERTS, tm), lambda i, j: (i, 0, j))],
        out_shape=[jax.ShapeDtypeStruct((b, s, d), F32),
                   jax.ShapeDtypeStruct((b, s, d), BF16),
                   jax.ShapeDtypeStruct((b, N_EXPERTS, s), F32)],
        compiler_params=_cparams("parallel", "parallel"),
        name="post_mixer",
    )(x, y, wo, mod, g2, router_t)


def _select_kernel(lg_ref, pos_ref, aff_ref, off_ref, *, cap, tt):
    lg = lg_ref[...]
    ne, t = lg.shape
    m = jnp.max(lg, axis=0, keepdims=True)
    ex = jnp.exp(lg - m)
    aff = ex / jnp.sum(ex, axis=0, keepdims=True)
    aff_ref[...] = aff
    bits = pltpu.bitcast(aff, I32)

    def search(i, prefix):
        cand = prefix | jnp.left_shift(jnp.int32(1), 30 - i)
        cnt = jnp.sum(jnp.where(bits >= cand, 1.0, 0.0), axis=1, keepdims=True)
        return jnp.where(cnt >= cap, cand, prefix)

    thr = lax.fori_loop(0, 31, search, jnp.zeros((ne, 1), I32))
    gt = bits > thr
    eq = bits == thr
    need = cap - jnp.sum(jnp.where(gt, 1.0, 0.0), axis=1, keepdims=True)

    nch = t // LANE
    tri = (lax.broadcasted_iota(I32, (LANE, LANE), 0) <= lax.broadcasted_iota(I32, (LANE, LANE), 1))
    tri = jnp.where(tri, 1.0, 0.0).astype(BF16)

    def chunk(a, c):
        return a[:, c * LANE:(c + 1) * LANE]

    off = jnp.zeros((ne, 1), F32)
    sel = []
    for c in range(nch):
        eqc = jnp.where(chunk(eq, c), 1.0, 0.0)
        inc = jnp.dot(eqc.astype(BF16), tri, preferred_element_type=F32)
        rank = inc - eqc + off
        off = off + inc[:, LANE - 1:LANE]
        sel.append(jnp.where(chunk(gt, c) | (chunk(eq, c) & (rank < need)), 1.0, 0.0))

    lane = lax.broadcasted_iota(I32, (ne, LANE), 1)
    offs = jnp.zeros((ne, LANE), F32)
    off = jnp.zeros((ne, 1), F32)
    per_tile = tt // LANE
    for c in range(nch):
        if c % per_tile == 0:
            offs = jnp.where(lane == c // per_tile, off, offs)
        inc = jnp.dot(sel[c].astype(BF16), tri, preferred_element_type=F32)
        pos = inc - sel[c] + off
        off = off + inc[:, LANE - 1:LANE]
        pos_ref[:, c * LANE:(c + 1) * LANE] = jnp.where(sel[c] > 0.0, pos, -1.0).astype(I32)
    offs = jnp.where(lane == nch // per_tile, off, offs)
    off_ref[...] = offs.astype(I32)


def _select(logits_t, t0_blk, t, cap):
    b, ne, _ = logits_t.shape
    tt = min(TOK_TILE, t)
    kern = functools.partial(_select_kernel, cap=cap, tt=tt)
    return pl.pallas_call(
        kern,
        grid=(b,),
        in_specs=[pl.BlockSpec((None, ne, t), lambda i: (i, 0, t0_blk))],
        out_specs=[pl.BlockSpec((None, ne, t), lambda i: (i, 0, 0)),
                   pl.BlockSpec((None, ne, t), lambda i: (i, 0, 0)),
                   pl.BlockSpec((None, ne, LANE), lambda i: (i, 0, 0))],
        out_shape=[jax.ShapeDtypeStruct((b, ne, t), I32),
                   jax.ShapeDtypeStruct((b, ne, t), F32),
                   jax.ShapeDtypeStruct((b, ne, LANE), I32)],
        compiler_params=_cparams("parallel"),
        name="moe_select",
    )(logits_t)


def _window_onehot(posr, lo, cap, w, rows):
    base = pl.multiple_of(jnp.minimum((lo // WIN_ALIGN) * WIN_ALIGN, cap - w), WIN_ALIGN)
    hit = ((posr - base) == rows) & (posr >= lo)
    return base, hit


def _tile_windows(off_sm, obase, stride, pos, cap, w, rows):
    out = []
    for e in range(pos.shape[0]):
        p0, p1 = off_sm[obase + e * stride], off_sm[obase + e * stride + 1]
        base, hit = _window_onehot(pos[e:e + 1], p0, cap, w, rows)
        out.append((p1, base, hit))
    return out


def _onehots(wins):
    return jnp.concatenate([jnp.where(hit, 1.0, 0.0).astype(BF16) for _, _, hit in wins], axis=0)


def _gather_kernel(off_sm, h_ref, pos_ref, aff_ref, xe_ref, gs_ref, *, cap, nt, w):
    bi, j = pl.program_id(0), pl.program_id(1)
    ne, tt = pos_ref.shape

    @pl.when(j == 0)
    def _():
        xe_ref[...] = jnp.zeros_like(xe_ref)
        gs_ref[...] = jnp.zeros_like(gs_ref)

    rows = lax.broadcasted_iota(I32, (w, tt), 0)
    pos, aff, ht = pos_ref[...], aff_ref[...], h_ref[...]
    obase = bi * ne * (nt + 1) + j

    def put(e, base, hit, picked):
        cur = xe_ref[e, pl.ds(base, w), :].astype(F32)
        xe_ref[e, pl.ds(base, w), :] = (cur + picked).astype(BF16)
        gs_ref[e, pl.ds(base, w), :] += jnp.sum(jnp.where(hit, aff[e:e + 1], 0.0), axis=1, keepdims=True)

    wins = _tile_windows(off_sm, obase, nt + 1, pos, cap, w, rows)
    picked = jnp.dot(_onehots(wins), ht, preferred_element_type=F32)
    for e, (p1, base, hit) in enumerate(wins):
        put(e, base, hit, picked[e * w:(e + 1) * w])

    for e, (p1, base, _) in enumerate(wins):

        @pl.when(p1 > base + w)
        def _():
            def more(lo):
                base2, hit2 = _window_onehot(pos[e:e + 1], lo, cap, w, rows)
                oh = jnp.where(hit2, 1.0, 0.0).astype(BF16)
                put(e, base2, hit2, jnp.dot(oh, ht, preferred_element_type=F32))
                return base2 + w

            lax.while_loop(lambda lo: lo < p1, more, base + w)


def _gather(off_flat, bm, pos, aff, t0_blk, t, cap):
    b, _, d = bm.shape
    ne = pos.shape[1]
    tt = min(TOK_TILE, t)
    nt = t // tt
    w = min(SLOT_WINDOW, cap)
    kern = functools.partial(_gather_kernel, cap=cap, nt=nt, w=w)
    grid_spec = pltpu.PrefetchScalarGridSpec(
        num_scalar_prefetch=1,
        grid=(b, nt),
        in_specs=[pl.BlockSpec((None, tt, d), lambda i, j, o: (i, t0_blk * nt + j, 0)),
                  pl.BlockSpec((None, ne, tt), lambda i, j, o: (i, 0, j)),
                  pl.BlockSpec((None, ne, tt), lambda i, j, o: (i, 0, j))],
        out_specs=[pl.BlockSpec((ne, None, cap, d), lambda i, j, o: (0, i, 0, 0)),
                   pl.BlockSpec((ne, None, cap, 1), lambda i, j, o: (0, i, 0, 0))])
    return pl.pallas_call(
        kern,
        grid_spec=grid_spec,
        out_shape=[jax.ShapeDtypeStruct((ne, b, cap, d), BF16),
                   jax.ShapeDtypeStruct((ne, b, cap, 1), F32)],
        compiler_params=_cparams("parallel", "arbitrary"),
        name="moe_gather",
    )(off_flat, bm, pos, aff)


def _ffn_kernel(x_ref, gs_ref, w1_ref, w3_ref, w2_ref, oh_ref, ol_ref, w1b, w3b, w2b):
    @pl.when(pl.program_id(1) == 0)
    def _():
        w1b[...] = w1_ref[...].astype(BF16)
        w3b[...] = w3_ref[...].astype(BF16)
        w2b[...] = w2_ref[...].astype(BF16)

    x = x_ref[...]
    h1 = jnp.dot(x, w1b[...], preferred_element_type=F32)
    h3 = jnp.dot(x, w3b[...], preferred_element_type=F32)
    hid = (h1 * _sigmoid(h1)) * h3
    ye = jnp.dot(hid.astype(BF16), w2b[...], preferred_element_type=F32) * gs_ref[...]
    oh_ref[...], ol_ref[...] = _split2(ye)


def _ffn(xe, gs, w1, w3, w2, layer):
    ne, r, d = xe.shape
    f = w1.shape[-1]
    tr = min(r, 512)
    return pl.pallas_call(
        _ffn_kernel,
        grid=(ne, r // tr),
        in_specs=[pl.BlockSpec((None, tr, d), lambda e, i: (e, i, 0)),
                  pl.BlockSpec((None, tr, 1), lambda e, i: (e, i, 0)),
                  pl.BlockSpec((None, None, d, f), lambda e, i: (layer, e, 0, 0)),
                  pl.BlockSpec((None, None, d, f), lambda e, i: (layer, e, 0, 0)),
                  pl.BlockSpec((None, None, f, d), lambda e, i: (layer, e, 0, 0))],
        out_specs=[pl.BlockSpec((None, tr, d), lambda e, i: (e, i, 0))] * 2,
        out_shape=[jax.ShapeDtypeStruct((ne, r, d), BF16)] * 2,
        scratch_shapes=[pltpu.VMEM((d, f), BF16), pltpu.VMEM((d, f), BF16), pltpu.VMEM((f, d), BF16)],
        compiler_params=_cparams("parallel", "arbitrary"),
        name="moe_ffn",
    )(xe, gs, w1, w3, w2)


def _combine_kernel(off_sm, x1_ref, mod_ref, pos_ref, yh_ref, yl_ref, o_ref, *, cap, nt, w):
    bi, j = pl.program_id(0), pl.program_id(2)
    ne, tt = pos_ref.shape
    rows = lax.broadcasted_iota(I32, (w, tt), 0)
    pos = pos_ref[...]
    wins = _tile_windows(off_sm, bi * ne * (nt + 1) + j, nt + 1, pos, cap, w, rows)
    onehot = _onehots(wins)
    tn = functools.partial(lax.dot_general, dimension_numbers=_TN, preferred_element_type=F32)
    slots = lambda ref: jnp.concatenate([ref[e, pl.ds(base, w), :] for e, (_, base, _) in enumerate(wins)], axis=0)
    o_ref[...] = tn(onehot, slots(yh_ref)) + tn(onehot, slots(yl_ref))

    for e, (p1, base, _) in enumerate(wins):

        @pl.when(p1 > base + w)
        def _():
            def more(lo):
                base2, hit2 = _window_onehot(pos[e:e + 1], lo, cap, w, rows)
                oh = jnp.where(hit2, 1.0, 0.0).astype(BF16)
                o_ref[...] += tn(oh, yh_ref[e, pl.ds(base2, w), :]) + tn(oh, yl_ref[e, pl.ds(base2, w), :])
                return base2 + w

            lax.while_loop(lambda lo: lo < p1, more, base + w)

    o_ref[...] = x1_ref[...] + mod_ref[...][5:6] * o_ref[...]


def _combine(off_flat, x1, mod, stream, pos, ye, t0_blk, t, cap):
    b, s, d = x1.shape
    ne = pos.shape[1]
    tt = min(TOK_TILE, t)
    nt = t // tt
    w = min(SLOT_WINDOW, cap)
    dh = d // COMBINE_SPLIT
    kern = functools.partial(_combine_kernel, cap=cap, nt=nt, w=w)
    grid_spec = pltpu.PrefetchScalarGridSpec(
        num_scalar_prefetch=1,
        grid=(b, COMBINE_SPLIT, nt),
        in_specs=[pl.BlockSpec((None, tt, dh), lambda i, c, j, o: (i, t0_blk * nt + j, c)),
                  pl.BlockSpec((None, None, 6, dh), lambda i, c, j, o: (i, stream, 0, c)),
                  pl.BlockSpec((None, ne, tt), lambda i, c, j, o: (i, 0, j)),
                  pl.BlockSpec((ne, None, cap, dh), lambda i, c, j, o: (0, i, 0, c)),
                  pl.BlockSpec((ne, None, cap, dh), lambda i, c, j, o: (0, i, 0, c))],
        out_specs=pl.BlockSpec((None, tt, dh), lambda i, c, j, o: (i, t0_blk * nt + j, c)))
    return pl.pallas_call(
        kern,
        grid_spec=grid_spec,
        out_shape=jax.ShapeDtypeStruct((b, s, d), F32),
        input_output_aliases={1: 0},
        compiler_params=_cparams("parallel", "parallel", "arbitrary"),
        name="moe_combine",
    )(off_flat, x1, mod, pos, *ye)


def _moe_stream(x1, bm, logits_t, mod, stream, t0_blk, t, experts):
    b = x1.shape[0]
    cap = CAPACITY_FACTOR * t // N_EXPERTS
    pos, aff, offs = _select(logits_t, t0_blk, t, cap)
    nt = t // min(TOK_TILE, t)
    off_flat = offs[:, :, :nt + 1].reshape(-1)
    xe, gs = _gather(off_flat, bm, pos, aff, t0_blk, t, cap)
    d = x1.shape[-1]
    ye = _ffn(xe.reshape(N_EXPERTS, b * cap, d), gs.reshape(N_EXPERTS, b * cap, 1), *experts)
    ye = [y.reshape(N_EXPERTS, b, cap, d) for y in ye]
    return _combine(off_flat, x1, mod, stream, pos, ye, t0_blk, t, cap)


def _rwkv_stream_kernel(*refs, t_lat, has_vres):
    if has_vres:
        (x_ref, xp_ref, xn_ref, mod_ref, g1_ref, mu_ref, wr_ref, wk_ref, wv_ref, w1_ref, w2_ref, w0_ref,
         a1_ref, a2_ref, a0_ref, gg1_ref, gg2_ref, kk_ref, ka_ref, rk_ref, seg_ref, segt_ref,
         vf_ref, v0_ref, v1_ref, v2_ref,
         r_out, v_out, kk_out, lw_out, kd_out, b_out, g_out, bonus_out) = refs
    else:
        (x_ref, xp_ref, xn_ref, mod_ref, g1_ref, mu_ref, wr_ref, wk_ref, wv_ref, w1_ref, w2_ref, w0_ref,
         a1_ref, a2_ref, a0_ref, gg1_ref, gg2_ref, kk_ref, ka_ref, rk_ref, seg_ref, segt_ref,
         r_out, v_out, kk_out, lw_out, kd_out, b_out, g_out, bonus_out) = refs
    j = pl.program_id(1)
    mod = mod_ref[...]
    g1 = g1_ref[...]
    nm = lambda x: _norm_mod(x, g1, mod[1:2], mod[0:1])
    h = nm(x_ref[...])
    tm, d = h.shape
    hp = nm(xp_ref[...])[SUBLANE - 1:SUBLANE]
    hn = nm(xn_ref[...])[0:1]
    row = lax.broadcasted_iota(I32, (tm, 1), 0)
    grow = row + j * tm
    s_tot = pl.num_programs(1) * tm
    first = (grow == 0) | (grow == t_lat)
    last = (grow == t_lat - 1) | (grow == s_tot - 1)
    prev = jnp.where(row == 0, hp, pltpu.roll(h, 1, 0))
    prev = jnp.where(first, 0.0, prev)
    nxt = jnp.where(row == tm - 1, hn, pltpu.roll(h, tm - 1, 0))
    nxt = jnp.where(last, 0.0, nxt)
    xx = 0.5 * (prev + nxt) - h
    mu = mu_ref[...]
    xs = [h + xx * mu[m:m + 1] for m in range(6)]
    r = _bdot(xs[0], wr_ref[...])
    k = _bdot(xs[1], wk_ref[...])
    v = _bdot(xs[2], wv_ref[...])
    if has_vres:
        gate = _sigmoid(v0_ref[...] + _bdot(_bdot(xs[2], v1_ref[...]), v2_ref[...]))
        v = v + (vf_ref[...] - v) * gate
    wz = w0_ref[...] + _bdot(jnp.tanh(_bdot(xs[3], w1_ref[...])), w2_ref[...])
    lw = -math.exp(-0.5) * _sigmoid(wz)
    a = _sigmoid(a0_ref[...] + _bdot(_bdot(xs[4], a1_ref[...]), a2_ref[...]))
    g = _bdot(_sigmoid(_bdot(xs[5], gg1_ref[...])), gg2_ref[...])
    seg, segt = seg_ref[...], segt_ref[...]
    segsum = lambda t: _dot_exact_rhs01(_dot_exact_rhs01(t, seg), segt)
    kkr = k * kk_ref[...]
    kk = kkr / jnp.maximum(jnp.sqrt(segsum(kkr * kkr)), 1e-12)
    ka = ka_ref[...]
    kd0 = k * (1.0 + (a[:, :d] - 1.0) * ka)
    kd1 = k * (1.0 + (a[:, d:] - 1.0) * ka)
    r_out[...] = r
    v_out[...] = v
    kk_out[...] = kk
    lw_out[...] = lw
    kd_out[:, :d] = kd0
    kd_out[:, d:] = kd1
    b_out[:, :d] = kk * a[:, :d]
    b_out[:, d:] = kk * a[:, d:]
    g_out[...] = g
    bonus_out[...] = segsum(r * (kd0 + kd1) * rk_ref[...]) * v


def _rwkv_stream(x, mod, g1, wts, vres, t_lat):
    b, s, d = x.shape
    tm = TOK_TILE
    nlt = t_lat // tm
    nsub = tm // SUBLANE
    const = lambda a: pl.BlockSpec(a.shape, lambda i, j: (0,) * a.ndim)
    tok = lambda w: pl.BlockSpec((None, tm, w), lambda i, j: (i, j, 0))
    in_specs = [tok(d),
                pl.BlockSpec((None, SUBLANE, d), lambda i, j: (i, jnp.maximum(j * nsub - 1, 0), 0)),
                pl.BlockSpec((None, SUBLANE, d), lambda i, j: (i, jnp.minimum((j + 1) * nsub, s // SUBLANE - 1), 0)),
                pl.BlockSpec((None, None, 6, d), lambda i, j: (i, (j >= nlt).astype(I32), 0, 0)),
                const(g1)] + [const(a) for a in wts]
    args = [x, x, x, mod, g1] + list(wts)
    if vres is not None:
        vf, v0, v1, v2 = vres
        in_specs += [tok(d), const(v0), const(v1), const(v2)]
        args += [vf, v0, v1, v2]
    kern = functools.partial(_rwkv_stream_kernel, t_lat=t_lat, has_vres=vres is not None)
    widths = [d, d, d, 2 * d, 2 * d, 2 * d, d, d]
    return pl.pallas_call(
        kern,
        grid=(b, s // tm),
        in_specs=in_specs,
        out_specs=[tok(w) for w in widths],
        out_shape=[jax.ShapeDtypeStruct((b, s, w), F32) for w in widths],
        compiler_params=_cparams("parallel", "parallel"),
        name="rwkv_stream",
    )(*args)


def _scan_kernel(r_ref, v_ref, kk_ref, lw_ref, kd_ref, b_ref, y_ref, h_ref, *, reverse, nchunk):
    L = SCAN_CHUNK

    @pl.when(pl.program_id(2) == 0)
    def _():
        h_ref[...] = jnp.zeros_like(h_ref)

    n2 = 2 * L
    rr = lax.broadcasted_iota(I32, (n2, n2), 0)
    cc = lax.broadcasted_iota(I32, (n2, n2), 1)
    tr, tc = rr % L, cc % L
    same = (rr // L) == (cc // L)
    before = (tr < tc) if reverse else (tr > tc)
    strict = same & before
    incl = same & (before | (tr == tc))
    eye = rr == cc
    blk16 = (rr // 16) == (cc // 16)
    blk32 = (rr // 32) == (cc // 32)
    ri = lax.broadcasted_iota(I32, (L, L), 0)
    ci = lax.broadcasted_iota(I32, (L, L), 1)
    tri = jnp.where((ci >= ri) if reverse else (ci <= ri), 1.0, 0.0).astype(BF16)
    head0 = lax.broadcasted_iota(I32, (L, LANE), 1) < RW_HEAD

    def stack(x):
        return jnp.concatenate([jnp.where(head0, x, 0.0), jnp.where(head0, 0.0, x)], axis=0)

    order = list(range(nchunk - 1, -1, -1) if reverse else range(nchunk))
    npp = y_ref.shape[-1] // LANE
    units = [(pp, ch) for pp in range(npp) for ch in order]

    def each(f, *lists):
        return [f(*a) for a in zip(*lists)]

    def ld(ref):
        return [ref[pl.ds(ch * L, L), pl.ds(pp * LANE, LANE)] for pp, ch in units]

    dd = functools.partial(jnp.dot, preferred_element_type=F32)
    lw = ld(lw_ref)
    lws = each(_split3, lw)
    cs = each(lambda t: dd(tri, t[0]) + (dd(tri, t[1]) + dd(tri, t[2])), lws)
    total = each(lambda c: c[0:1] if reverse else c[L - 1:L], cs)
    gam = each(jnp.exp, cs)
    ginv = each(lambda c: jnp.exp(-c), cs)
    gprev = each(lambda c, l: jnp.exp(c - l), cs, lw)
    gend = each(lambda t, c: jnp.exp(t - c), total, cs)
    kk, bb, kd, r, v = ld(kk_ref), ld(b_ref), ld(kd_ref), ld(r_ref), ld(v_ref)
    mul_stack = lambda a, g: stack(a * g)
    rtm = each(mul_stack, r, gam)
    rts = each(_split2, rtm)
    kkm, btm, ktm = (each(_split2, each(mul_stack, a, g)) for a, g in ((kk, gprev), (bb, ginv), (kd, ginv)))
    bht, kht = (each(lambda a, g: _split2(stack(a * g).T), a, gend) for a in (bb, kd))
    vm = each(_split2, each(stack, v))
    n_ab = each(lambda a, c, e: _mm3_rows([a, c], e, _NT), kkm, rts, btm)
    m_ak = each(lambda a, c, e: _mm3_rows([a, c], e, _NT), kkm, rts, ktm)
    n = each(lambda t: jnp.where(strict, t[0], 0.0), n_ab)
    ab = each(lambda t: jnp.where(incl, t[1], 0.0), n_ab)
    m = each(lambda t: jnp.where(strict, t[0], 0.0), m_ak)
    ak = each(lambda t: jnp.where(incl, t[1], 0.0), m_ak)
    nd = each(lambda t: jnp.where(blk16, t, 0.0), n)
    x = each(lambda t: jnp.where(eye, 1.0, 0.0) - t, nd)
    pw = nd
    for _ in range(3):
        pw = each(lambda t: _bdot(t, t), pw)
        x = each(lambda a, c: a + _bdot(a, c), x, pw)
    ident = jnp.where(eye, 1.0, 0.0)
    for inner, outer in ((blk16, blk32), (blk32, same)):
        c = each(lambda t: jnp.where(outer & jnp.logical_not(inner), t, 0.0), n)
        e = each(lambda a: a - ident, x)
        ec = each(_bdot, e, c)
        x = each(lambda a, t, u, s: a - t - (u + _bdot(t + u, s)), x, c, ec, e)
    xs = each(_split2, x)
    abs_ = each(_split2, ab)
    mv_akv_khv = each(lambda a, c, e, f: _mm3_rows([_split2(a), _split2(c), e], f), m, ak, kht, vm)
    cat = lambda a, c: tuple(jnp.concatenate([s, t], axis=1) for s, t in zip(a, c))
    wu = each(lambda a, c, t: _split2(_mm3(a, cat(c, _split2(t[0])))), xs, kkm, mv_akv_khv)
    bw_abw = each(lambda a, c, e: _mm3_rows([a, c], e), bht, abs_, wu)
    p = each(lambda t, a: jnp.where(eye, jnp.exp(t), 0.0) - a[0][:, :LANE], total, bw_abw)
    q = each(lambda t, a: t[2] - a[0][:, LANE:], mv_akv_khv, bw_abw)
    rres = each(lambda t, a: t - a[1][:, :LANE], rtm, bw_abw)
    y0 = each(lambda t, a: t[1] - a[1][:, LANE:], mv_akv_khv, bw_abw)
    for i, (pp, ch) in enumerate(units):
        y, hn = _mm3_rows([_split2(rres[i]), _split2(p[i])], _split2(h_ref[pp]))
        y = y + y0[i]
        y_ref[pl.ds(ch * L, L), pl.ds(pp * LANE, LANE)] = y[:L] + y[L:]
        h_ref[pp] = hn + q[i]


def _scan(r, v, kk, lw, kd, bb, t_lat, reverse):
    b, s, d = r.shape
    tb = TOK_TILE
    nl, nc = t_lat // tb, (s - t_lat) // tb
    z = 1 if reverse else 0

    def blk(c):
        if reverse:
            return jnp.where(c < nc, nl + nc - 1 - c, nl - 1 - (c - nc))
        return jnp.where(c < nc, nl + c, c - nc)

    wl = SCAN_PAIRS * LANE
    ngrp = d // wl
    shared = pl.BlockSpec((None, tb, wl), lambda i, p, c: (i, blk(c), p))
    dirn = pl.BlockSpec((None, tb, wl), lambda i, p, c: (i, blk(c), z * ngrp + p))
    kern = functools.partial(_scan_kernel, reverse=reverse, nchunk=tb // SCAN_CHUNK)
    return pl.pallas_call(
        kern,
        grid=(b, ngrp, nl + nc),
        in_specs=[shared, shared, shared, dirn, dirn, dirn],
        out_specs=pl.BlockSpec((None, tb, wl), lambda i, p, c: (i, blk(c), p)),
        out_shape=jax.ShapeDtypeStruct((b, s, d), F32),
        scratch_shapes=[pltpu.VMEM((SCAN_PAIRS, LANE, LANE), F32)],
        compiler_params=_cparams("parallel", "parallel", "arbitrary"),
        name="wkv_scan_bwd" if reverse else "wkv_scan_fwd",
    )(r, v, kk, lw, kd, bb)


def _rwkv_out_kernel(yf_ref, yb_ref, bonus_ref, g_ref, lg_ref, lb_ref, seg_ref, segt_ref, o_ref):
    seg, segt = seg_ref[...], segt_ref[...]
    segmean = lambda t: _dot_exact_rhs01(_dot_exact_rhs01(t, seg), segt) * (1.0 / RW_HEAD)
    y = yf_ref[...] + yb_ref[...]
    dlt = y - segmean(y)
    var = segmean(dlt * dlt)
    yn = (dlt * lax.rsqrt(var + RW_LNX_EPS)) * lg_ref[...] + lb_ref[...]
    o_ref[...] = ((yn + bonus_ref[...]) * g_ref[...]).astype(BF16)


def _rwkv_out(yf, yb, bonus, g, lnx_g, lnx_b, seg, segt):
    b, s, d = yf.shape
    tm = TOK_TILE
    tok = pl.BlockSpec((None, tm, d), lambda i, j: (i, j, 0))
    const = lambda a: pl.BlockSpec(a.shape, lambda i, j: (0,) * a.ndim)
    return pl.pallas_call(
        _rwkv_out_kernel,
        grid=(b, s // tm),
        in_specs=[tok, tok, tok, tok, const(lnx_g), const(lnx_b), const(seg), const(segt)],
        out_specs=tok,
        out_shape=jax.ShapeDtypeStruct((b, s, d), BF16),
        compiler_params=_cparams("parallel", "parallel"),
        name="rwkv_out",
    )(yf, yb, bonus, g, lnx_g, lnx_b, seg, segt)


def _final_kernel(x_ref, g_ref, o_ref):
    o_ref[...] = _rms(x_ref[...], g_ref[...])


def _final_norm(x, g, t_lat):
    b, _, d = x.shape
    tm = TOK_TILE
    return pl.pallas_call(
        _final_kernel,
        grid=(b, t_lat // tm),
        in_specs=[pl.BlockSpec((None, tm, d), lambda i, j: (i, j, 0)),
                  pl.BlockSpec(g.shape, lambda i, j: (0, 0))],
        out_specs=pl.BlockSpec((None, tm, d), lambda i, j: (i, j, 0)),
        out_shape=jax.ShapeDtypeStruct((b, t_lat, d), F32),
        compiler_params=_cparams("parallel", "parallel"),
        name="final_norm",
    )(x, g)


def _rope_tables(t_lat, t_ctx):
    rows = t_lat // GRID_W
    r = jnp.broadcast_to(jnp.arange(rows)[:, None], (rows, GRID_W)).reshape(-1).astype(F32)
    col = jnp.broadcast_to(jnp.arange(GRID_W)[None, :], (rows, GRID_W)).reshape(-1).astype(F32)
    n_freq = MLA_ROPE // 4
    inv = ROPE_THETA ** (-jnp.arange(n_freq, dtype=F32) / n_freq)
    ang = jnp.concatenate([r[:, None] * inv, col[:, None] * inv], axis=-1)
    cos, sin = jnp.cos(ang), jnp.sin(ang)
    cosf = jnp.concatenate([cos, cos, cos, cos], axis=-1)
    sins = jnp.concatenate([-sin, sin, -sin, sin], axis=-1)
    cosf = jnp.concatenate([cosf, jnp.ones((t_ctx, LANE), F32)], axis=0)
    sins = jnp.concatenate([sins, jnp.zeros((t_ctx, LANE), F32)], axis=0)
    return cosf, sins


def _blockdiag2(w):
    z = jnp.zeros_like(w[0])
    return jnp.concatenate([jnp.concatenate([w[0], z], axis=1), jnp.concatenate([z, w[1]], axis=1)], axis=0)


def kernel(x, c, ctx, c_ctx, mod_w, mod_b, norm1_g, norm2_g, final_g, mla_w_in, mla_g_q, mla_g_kv, mla_w_uq, mla_w_ukv, mla_w_o, rw_mu, rw_w_rkv, rw_w0, rw_w1, rw_w2, rw_a0, rw_a1, rw_a2, rw_v0, rw_v1, rw_v2, rw_g1, rw_g2, rw_k_k, rw_k_a, rw_r_k, rw_lnx_g, rw_lnx_b, rw_w_o, moe_router, moe_w1, moe_w3, moe_w2):
    b, t_lat, d = x.shape
    t_ctx = ctx.shape[1]
    depth = mod_w.shape[0]
    assert b + 1 <= SUBLANE and t_lat % TOK_TILE == 0 and t_ctx % TOK_TILE == 0 and d % LANE == 0
    n_lat_tiles = t_lat // TOK_TILE
    ctx_blk = t_lat // t_ctx
    assert ctx_blk * t_ctx == t_lat

    cc = jnp.concatenate([c, c_ctx[None], jnp.zeros((SUBLANE - b - 1, d), F32)], axis=0)
    mods = _modulation(cc, mod_w, mod_b).reshape(depth, SUBLANE, 6, d)
    mod_all = jnp.stack([mods[:, :b], jnp.broadcast_to(mods[:, b:b + 1], (depth, b, 6, d))], axis=2)

    xs = jnp.concatenate([x, ctx], axis=1)
    cosf, sins = _rope_tables(t_lat, t_ctx)
    row = lambda a: a.reshape(1, -1)
    seg = (jnp.arange(d)[:, None] // RW_HEAD == jnp.arange(LANE)[None, :]).astype(BF16)
    segt = seg.T

    v_first = None
    for layer in range(depth):
        need_ctx = layer < depth - 1
        mod = mod_all[layer]
        j = layer // 2
        if layer % 2 == 0:
            win = jnp.pad(mla_w_in[j], ((0, 0), (0, 4 * LANE - mla_w_in.shape[-1]))).astype(BF16)
            wuq = mla_w_uq[j].reshape(MLA_Q_LORA, MLA_HEADS, MLA_NOPE + MLA_ROPE)
            wuq = jnp.concatenate([wuq[..., :MLA_NOPE].reshape(MLA_Q_LORA, -1),
                                   wuq[..., MLA_NOPE:].reshape(MLA_Q_LORA, -1)], axis=1).astype(BF16)
            q, k, v = _mla_proj(xs, mod, row(norm1_g[layer]), win, row(mla_g_q[j]), row(mla_g_kv[j]),
                                wuq, mla_w_ukv[j].astype(BF16), cosf, sins, n_lat_tiles)
            y = _attention(q, k, v, t_lat)
            wo = mla_w_o[j].astype(BF16)
        else:
            lora = lambda w: jnp.concatenate([w[0], w[1]], axis=1).astype(BF16)
            wts = [rw_mu[j], rw_w_rkv[j, 0].astype(BF16), rw_w_rkv[j, 1].astype(BF16), rw_w_rkv[j, 2].astype(BF16),
                   lora(rw_w1[j]), _blockdiag2(rw_w2[j]).astype(BF16), rw_w0[j].reshape(1, -1),
                   lora(rw_a1[j]), _blockdiag2(rw_a2[j]).astype(BF16), rw_a0[j].reshape(1, -1),
                   rw_g1[j].astype(BF16), rw_g2[j].astype(BF16),
                   row(rw_k_k[j]), row(rw_k_a[j]), row(rw_r_k[j]), seg, segt]
            vres = None
            if j > 0:
                pad = LANE - rw_v1.shape[-1]
                vres = (v_first, row(rw_v0[j - 1]), jnp.pad(rw_v1[j - 1], ((0, 0), (0, pad))).astype(BF16),
                        jnp.pad(rw_v2[j - 1], ((0, pad), (0, 0))).astype(BF16))
            r, v, kk, lw, kd, bb, g, bonus = _rwkv_stream(xs, mod, row(norm1_g[layer]), wts, vres, t_lat)
            if j == 0:
                v_first = v
            yf = _scan(r, v, kk, lw, kd, bb, t_lat, reverse=False)
            yb = _scan(r, v, kk, lw, kd, bb, t_lat, reverse=True)
            y = _rwkv_out(yf, yb, bonus, g, row(rw_lnx_g[j]), row(rw_lnx_b[j]), seg, segt)
            wo = rw_w_o[j].astype(BF16)
        x1, bm, logits_t = _post_mixer(xs, y, wo, mod, row(norm2_g[layer]), moe_router[layer].T.astype(BF16),
                                       n_lat_tiles)
        experts = (moe_w1, moe_w3, moe_w2, layer)
        xs = _moe_stream(x1, bm, logits_t, mod, 0, 0, t_lat, experts)
        if need_ctx:
            xs = _moe_stream(xs, bm, logits_t, mod, 1, ctx_blk, t_ctx, experts)
    return _final_norm(xs, row(final_g), t_lat)
```

```python
import functools
import math

import jax
import jax.numpy as jnp
from jax import lax
from jax.experimental import pallas as pl
from jax.experimental.pallas import tpu as pltpu

F32, BF16, I32 = jnp.float32, jnp.bfloat16, jnp.int32

GRID_W = 64
MLA_HEADS, MLA_NOPE, MLA_ROPE, MLA_V = 8, 128, 64, 128
MLA_Q_LORA, MLA_KV_LORA = 256, 128
MLA_SCALE = (MLA_NOPE + MLA_ROPE) ** -0.5
ROPE_THETA = 10000.0
RW_HEAD = 64
RW_LNX_EPS = 64e-5
N_EXPERTS = 16
CAPACITY_FACTOR = 2
RMS_EPS = 1e-6

LANE = 128
SUBLANE = 8
TOK_TILE = 256
WIN_ALIGN = 16
SLOT_WINDOW = 64
COMBINE_SPLIT = 2
ATTN_TILE = 1024
ATTN_ROWS = 256
SCAN_CHUNK = 64
SCAN_PAIRS = 4
VMEM_LIMIT = 56 * 1024 * 1024


def _cparams(*sem):
    return pltpu.CompilerParams(dimension_semantics=sem, vmem_limit_bytes=VMEM_LIMIT)


def _bdot(a, b):
    return jnp.dot(a.astype(BF16), b.astype(BF16), preferred_element_type=F32)


_NT = (((1,), (1,)), ((), ()))
_TN = (((0,), (0,)), ((), ()))
_NN = (((1,), (0,)), ((), ()))


def _split2(x):
    hi = x.astype(BF16)
    lo = (x - hi.astype(F32)).astype(BF16)
    return hi, lo


def _split3(x):
    hi = x.astype(BF16)
    r = x - hi.astype(F32)
    mid = r.astype(BF16)
    lo = (r - mid.astype(F32)).astype(BF16)
    return hi, mid, lo


def _mm3(a, b, dims=_NN):
    (lc,), (rc,) = dims[0]
    lhs = jnp.concatenate([a[0], a[1], a[0]], axis=lc)
    rhs = jnp.concatenate([b[0], b[0], b[1]], axis=rc)
    return lax.dot_general(lhs, rhs, dims, preferred_element_type=F32)


def _mm3_rows(lhs, b, dims=_NN):
    his = jnp.concatenate([a[0] for a in lhs], axis=0)
    los = jnp.concatenate([a[1] for a in lhs], axis=0)
    res = _mm3((his, los), b, dims)
    outs, o = [], 0
    for a in lhs:
        outs.append(res[o:o + a[0].shape[0]])
        o += a[0].shape[0]
    return outs


def _dot_exact_rhs01(x, m01):
    h, m, l = _split3(x)
    d = functools.partial(jnp.dot, preferred_element_type=F32)
    return d(h, m01) + (d(m, m01) + d(l, m01))


def _rms(x, g):
    return (x * lax.rsqrt(jnp.mean(x * x, axis=-1, keepdims=True) + RMS_EPS)) * g


def _norm_mod(x, g, scale, shift):
    return _rms(x, g) * (1.0 + scale) + shift


def _sigmoid(x):
    return 1.0 / (1.0 + jnp.exp(-x))


def _mod_kernel(c_ref, w_ref, b_ref, o_ref):
    s = c_ref[...]
    s = s * _sigmoid(s)
    o_ref[...] = _bdot(s, w_ref[...]) + b_ref[...]


def _modulation(cc, mod_w, mod_b):
    depth, d, n = mod_w.shape
    tn = n // 4
    return pl.pallas_call(
        _mod_kernel,
        grid=(depth, n // tn),
        in_specs=[pl.BlockSpec((SUBLANE, d), lambda l, j: (0, 0)),
                  pl.BlockSpec((None, d, tn), lambda l, j: (l, 0, j)),
                  pl.BlockSpec((None, 1, tn), lambda l, j: (l, 0, j))],
        out_specs=pl.BlockSpec((None, SUBLANE, tn), lambda l, j: (l, 0, j)),
        out_shape=jax.ShapeDtypeStruct((depth, SUBLANE, n), F32),
        compiler_params=_cparams("arbitrary", "arbitrary"),
        name="modulation",
    )(cc, mod_w, mod_b.reshape(depth, 1, n))


def _rope128(x, cosf, sins):
    lane = lax.broadcasted_iota(I32, x.shape, 1)
    first = (lane % MLA_ROPE) < (MLA_ROPE // 2)
    rot = jnp.where(first, pltpu.roll(x, LANE - MLA_ROPE // 2, 1), pltpu.roll(x, MLA_ROPE // 2, 1))
    return x * cosf + rot * sins


def _mla_proj_kernel(x_ref, mod_ref, g1_ref, win_ref, gq_ref, gkv_ref, wuq_ref, wukv_ref,
                     cos_ref, sin_ref, q_ref, k_ref, v_ref):
    mod = mod_ref[...]
    h = _norm_mod(x_ref[...], g1_ref[...], mod[1:2], mod[0:1])
    z = _bdot(h, win_ref[...])
    cq = _rms(z[:, :MLA_Q_LORA], gq_ref[...])
    ckv = _rms(z[:, MLA_Q_LORA:MLA_Q_LORA + MLA_KV_LORA], gkv_ref[...])
    kr = z[:, MLA_Q_LORA + MLA_KV_LORA:]
    q = _bdot(cq, wuq_ref[...])
    kv = _bdot(ckv, wukv_ref[...])
    cosf, sins = cos_ref[...], sin_ref[...]
    kr_lo = _rope128(kr, cosf, sins)
    kr_hi = pltpu.roll(kr_lo, MLA_ROPE, 1)
    nope_w = MLA_HEADS * MLA_NOPE
    for g in range(MLA_HEADS // 2):
        qr = _rope128(q[:, nope_w + LANE * g:nope_w + LANE * (g + 1)], cosf, sins)
        for hh in (2 * g, 2 * g + 1):
            qn = q[:, MLA_NOPE * hh:MLA_NOPE * (hh + 1)]
            q_ref[hh] = jnp.concatenate([qn, qr], axis=1).astype(BF16)
    for hh in range(MLA_HEADS):
        base = (MLA_NOPE + MLA_V) * hh
        krh = kr_lo if hh % 2 == 0 else kr_hi
        k_ref[hh] = jnp.concatenate([kv[:, base:base + MLA_NOPE], krh], axis=1).astype(BF16)
        v_ref[hh] = kv[:, base + MLA_NOPE:base + MLA_NOPE + MLA_V].astype(BF16)


def _mla_proj(x, mod, g1, win, gq, gkv, wuq, wukv, cosf, sins, n_lat_tiles):
    b, s, d = x.shape
    tm = TOK_TILE
    const = lambda shape: pl.BlockSpec(shape, lambda i, j: (0,) * len(shape))
    hspec = lambda w: pl.BlockSpec((None, MLA_HEADS, tm, w), lambda i, j: (i, 0, j, 0))
    return pl.pallas_call(
        _mla_proj_kernel,
        grid=(b, s // tm),
        in_specs=[pl.BlockSpec((None, tm, d), lambda i, j: (i, j, 0)),
                  pl.BlockSpec((None, None, 6, d), lambda i, j: (i, (j >= n_lat_tiles).astype(I32), 0, 0)),
                  const(g1.shape), const(win.shape), const(gq.shape), const(gkv.shape),
                  const(wuq.shape), const(wukv.shape),
                  pl.BlockSpec((tm, LANE), lambda i, j: (j, 0)),
                  pl.BlockSpec((tm, LANE), lambda i, j: (j, 0))],
        out_specs=[hspec(2 * LANE), hspec(2 * LANE), hspec(MLA_V)],
        out_shape=[jax.ShapeDtypeStruct((b, MLA_HEADS, s, 2 * LANE), BF16),
                   jax.ShapeDtypeStruct((b, MLA_HEADS, s, 2 * LANE), BF16),
                   jax.ShapeDtypeStruct((b, MLA_HEADS, s, MLA_V), BF16)],
        compiler_params=_cparams("parallel", "parallel"),
        name="mla_proj",
    )(x, mod, g1, win, gq, gkv, wuq, wukv, cosf, sins)


def _attn_kernel(q_ref, k_ref, v_ref, o_ref, *, n_lat_tiles, t_lat):
    j = pl.program_id(2)
    t_ctx = k_ref.shape[0] - t_lat

    def softmax_pv(s, v):
        m = jnp.max(s, axis=-1, keepdims=True)
        p = jnp.exp2((s - m) * (MLA_SCALE * math.log2(math.e)))
        l = jnp.sum(p, axis=-1, keepdims=True)
        o = jnp.dot(p.astype(BF16), v, preferred_element_type=F32)
        return (o / l).astype(o_ref.dtype)

    @pl.when(j < n_lat_tiles)
    def _():
        for r0 in range(0, o_ref.shape[0], ATTN_ROWS):
            s = lax.dot_general(q_ref[r0:r0 + ATTN_ROWS, :], k_ref[...], _NT, preferred_element_type=F32)
            o_ref[r0:r0 + ATTN_ROWS, :] = softmax_pv(s, v_ref[...])

    @pl.when(j >= n_lat_tiles)
    def _():
        s = lax.dot_general(q_ref[:t_ctx, :], k_ref[t_lat:, :], _NT, preferred_element_type=F32)
        o_ref[:t_ctx, :] = softmax_pv(s, v_ref[t_lat:, :])


def _attention(q, k, v, t_lat):
    b, nh, s, _ = q.shape
    tq = min(ATTN_TILE, t_lat)
    assert t_lat % tq == 0 and s - t_lat <= tq
    kern = functools.partial(_attn_kernel, n_lat_tiles=t_lat // tq, t_lat=t_lat)
    return pl.pallas_call(
        kern,
        grid=(b, nh, pl.cdiv(s, tq)),
        in_specs=[pl.BlockSpec((None, None, tq, q.shape[-1]), lambda i, h, j: (i, h, j, 0)),
                  pl.BlockSpec((None, None, s, k.shape[-1]), lambda i, h, j: (i, h, 0, 0)),
                  pl.BlockSpec((None, None, s, v.shape[-1]), lambda i, h, j: (i, h, 0, 0))],
        out_specs=pl.BlockSpec((None, tq, MLA_V), lambda i, h, j: (i, j, h)),
        out_shape=jax.ShapeDtypeStruct((b, s, nh * MLA_V), BF16),
        compiler_params=_cparams("parallel", "parallel", "arbitrary"),
        name="mla_attention",
    )(q, k, v)


def _post_mixer_kernel(x_ref, y_ref, wo_ref, mod_ref, g2_ref, rt_ref, x1_ref, bm_ref, lg_ref):
    mod = mod_ref[...]
    x1 = x_ref[...] + mod[2:3] * jnp.dot(y_ref[...], wo_ref[...], preferred_element_type=F32)
    x1_ref[...] = x1
    bm = _norm_mod(x1, g2_ref[...], mod[4:5], mod[3:4]).astype(BF16)
    bm_ref[...] = bm
    lg_ref[...] = lax.dot_general(rt_ref[...], bm, _NT, preferred_element_type=F32)


def _post_mixer(x, y, wo, mod, g2, router_t, n_lat_tiles):
    b, s, d = x.shape
    tm = TOK_TILE
    const = lambda shape: pl.BlockSpec(shape, lambda i, j: (0,) * len(shape))
    return pl.pallas_call(
        _post_mixer_kernel,
        grid=(b, s // tm),
        in_specs=[pl.BlockSpec((None, tm, d), lambda i, j: (i, j, 0)),
                  pl.BlockSpec((None, tm, y.shape[-1]), lambda i, j: (i, j, 0)),
                  const(wo.shape),
                  pl.BlockSpec((None, None, 6, d), lambda i, j: (i, (j >= n_lat_tiles).astype(I32), 0, 0)),
                  const(g2.shape), const(router_t.shape)],
        out_specs=[pl.BlockSpec((None, tm, d), lambda i, j: (i, j, 0)),
                   pl.BlockSpec((None, tm, d), lambda i, j: (i, j, 0)),
                   pl.BlockSpec((None, N_EXPERTS, tm), lambda i, j: (i, 0, j))],
        out_shape=[jax.ShapeDtypeStruct((b, s, d), F32),
                   jax.ShapeDtypeStruct((b, s, d), BF16),
                   jax.ShapeDtypeStruct((b, N_EXPERTS, s), F32)],
        compiler_params=_cparams("parallel", "parallel"),
        name="post_mixer",
    )(x, y, wo, mod, g2, router_t)


def _select_kernel(lg_ref, pos_ref, aff_ref, off_ref, *, cap, tt):
    lg = lg_ref[...]
    ne, t = lg.shape
    m = jnp.max(lg, axis=0, keepdims=True)
    ex = jnp.exp(lg - m)
    aff = ex / jnp.sum(ex, axis=0, keepdims=True)
    aff_ref[...] = aff
    bits = pltpu.bitcast(aff, I32)

    def search(i, prefix):
        cand = prefix | jnp.left_shift(jnp.int32(1), 30 - i)
        cnt = jnp.sum(jnp.where(bits >= cand, 1.0, 0.0), axis=1, keepdims=True)
        return jnp.where(cnt >= cap, cand, prefix)

    thr = lax.fori_loop(0, 31, search, jnp.zeros((ne, 1), I32))
    gt = bits > thr
    eq = bits == thr
    need = cap - jnp.sum(jnp.where(gt, 1.0, 0.0), axis=1, keepdims=True)

    nch = t // LANE
    tri = (lax.broadcasted_iota(I32, (LANE, LANE), 0) <= lax.broadcasted_iota(I32, (LANE, LANE), 1))
    tri = jnp.where(tri, 1.0, 0.0).astype(BF16)

    def chunk(a, c):
        return a[:, c * LANE:(c + 1) * LANE]

    off = jnp.zeros((ne, 1), F32)
    sel = []
    for c in range(nch):
        eqc = jnp.where(chunk(eq, c), 1.0, 0.0)
        inc = jnp.dot(eqc.astype(BF16), tri, preferred_element_type=F32)
        rank = inc - eqc + off
        off = off + inc[:, LANE - 1:LANE]
        sel.append(jnp.where(chunk(gt, c) | (chunk(eq, c) & (rank < need)), 1.0, 0.0))

    lane = lax.broadcasted_iota(I32, (ne, LANE), 1)
    offs = jnp.zeros((ne, LANE), F32)
    off = jnp.zeros((ne, 1), F32)
    per_tile = tt // LANE
    for c in range(nch):
        if c % per_tile == 0:
            offs = jnp.where(lane == c // per_tile, off, offs)
        inc = jnp.dot(sel[c].astype(BF16), tri, preferred_element_type=F32)
        pos = inc - sel[c] + off
        off = off + inc[:, LANE - 1:LANE]
        pos_ref[:, c * LANE:(c + 1) * LANE] = jnp.where(sel[c] > 0.0, pos, -1.0).astype(I32)
    offs = jnp.where(lane == nch // per_tile, off, offs)
    off_ref[...] = offs.astype(I32)


def _select(logits_t, t0_blk, t, cap):
    b, ne, _ = logits_t.shape
    tt = min(TOK_TILE, t)
    kern = functools.partial(_select_kernel, cap=cap, tt=tt)
    return pl.pallas_call(
        kern,
        grid=(b,),
        in_specs=[pl.BlockSpec((None, ne, t), lambda i: (i, 0, t0_blk))],
        out_specs=[pl.BlockSpec((None, ne, t), lambda i: (i, 0, 0)),
                   pl.BlockSpec((None, ne, t), lambda i: (i, 0, 0)),
                   pl.BlockSpec((None, ne, LANE), lambda i: (i, 0, 0))],
        out_shape=[jax.ShapeDtypeStruct((b, ne, t), I32),
                   jax.ShapeDtypeStruct((b, ne, t), F32),
                   jax.ShapeDtypeStruct((b, ne, LANE), I32)],
        compiler_params=_cparams("parallel"),
        name="moe_select",
    )(logits_t)


def _window_onehot(posr, lo, cap, w, rows):
    base = pl.multiple_of(jnp.minimum((lo // WIN_ALIGN) * WIN_ALIGN, cap - w), WIN_ALIGN)
    hit = ((posr - base) == rows) & (posr >= lo)
    return base, hit


def _tile_windows(off_sm, obase, stride, pos, cap, w, rows):
    out = []
    for e in range(pos.shape[0]):
        p0, p1 = off_sm[obase + e * stride], off_sm[obase + e * stride + 1]
        base, hit = _window_onehot(pos[e:e + 1], p0, cap, w, rows)
        out.append((p1, base, hit))
    return out


def _onehots(wins):
    return jnp.concatenate([jnp.where(hit, 1.0, 0.0).astype(BF16) for _, _, hit in wins], axis=0)


def _gather_kernel(off_sm, h_ref, pos_ref, aff_ref, xe_ref, gs_ref, *, cap, nt, w):
    bi, j = pl.program_id(0), pl.program_id(1)
    ne, tt = pos_ref.shape

    @pl.when(j == 0)
    def _():
        xe_ref[...] = jnp.zeros_like(xe_ref)
        gs_ref[...] = jnp.zeros_like(gs_ref)

    rows = lax.broadcasted_iota(I32, (w, tt), 0)
    pos, aff, ht = pos_ref[...], aff_ref[...], h_ref[...]
    obase = bi * ne * (nt + 1) + j

    def put(e, base, hit, picked):
        cur = xe_ref[e, pl.ds(base, w), :].astype(F32)
        xe_ref[e, pl.ds(base, w), :] = (cur + picked).astype(BF16)
        gs_ref[e, pl.ds(base, w), :] += jnp.sum(jnp.where(hit, aff[e:e + 1], 0.0), axis=1, keepdims=True)

    wins = _tile_windows(off_sm, obase, nt + 1, pos, cap, w, rows)
    picked = jnp.dot(_onehots(wins), ht, preferred_element_type=F32)
    for e, (p1, base, hit) in enumerate(wins):
        put(e, base, hit, picked[e * w:(e + 1) * w])

    for e, (p1, base, _) in enumerate(wins):

        @pl.when(p1 > base + w)
        def _():
            def more(lo):
                base2, hit2 = _window_onehot(pos[e:e + 1], lo, cap, w, rows)
                oh = jnp.where(hit2, 1.0, 0.0).astype(BF16)
                put(e, base2, hit2, jnp.dot(oh, ht, preferred_element_type=F32))
                return base2 + w

            lax.while_loop(lambda lo: lo < p1, more, base + w)


def _gather(off_flat, bm, pos, aff, t0_blk, t, cap):
    b, _, d = bm.shape
    ne = pos.shape[1]
    tt = min(TOK_TILE, t)
    nt = t // tt
    w = min(SLOT_WINDOW, cap)
    kern = functools.partial(_gather_kernel, cap=cap, nt=nt, w=w)
    grid_spec = pltpu.PrefetchScalarGridSpec(
        num_scalar_prefetch=1,
        grid=(b, nt),
        in_specs=[pl.BlockSpec((None, tt, d), lambda i, j, o: (i, t0_blk * nt + j, 0)),
                  pl.BlockSpec((None, ne, tt), lambda i, j, o: (i, 0, j)),
                  pl.BlockSpec((None, ne, tt), lambda i, j, o: (i, 0, j))],
        out_specs=[pl.BlockSpec((ne, None, cap, d), lambda i, j, o: (0, i, 0, 0)),
                   pl.BlockSpec((ne, None, cap, 1), lambda i, j, o: (0, i, 0, 0))])
    return pl.pallas_call(
        kern,
        grid_spec=grid_spec,
        out_shape=[jax.ShapeDtypeStruct((ne, b, cap, d), BF16),
                   jax.ShapeDtypeStruct((ne, b, cap, 1), F32)],
        compiler_params=_cparams("parallel", "arbitrary"),
        name="moe_gather",
    )(off_flat, bm, pos, aff)


def _ffn_kernel(x_ref, gs_ref, w1_ref, w3_ref, w2_ref, oh_ref, ol_ref, w1b, w3b, w2b):
    @pl.when(pl.program_id(1) == 0)
    def _():
        w1b[...] = w1_ref[...].astype(BF16)
        w3b[...] = w3_ref[...].astype(BF16)
        w2b[...] = w2_ref[...].astype(BF16)

    x = x_ref[...]
    h1 = jnp.dot(x, w1b[...], preferred_element_type=F32)
    h3 = jnp.dot(x, w3b[...], preferred_element_type=F32)
    hid = (h1 * _sigmoid(h1)) * h3
    ye = jnp.dot(hid.astype(BF16), w2b[...], preferred_element_type=F32) * gs_ref[...]
    oh_ref[...], ol_ref[...] = _split2(ye)


def _ffn(xe, gs, w1, w3, w2, layer):
    ne, r, d = xe.shape
    f = w1.shape[-1]
    tr = min(r, 512)
    return pl.pallas_call(
        _ffn_kernel,
        grid=(ne, r // tr),
        in_specs=[pl.BlockSpec((None, tr, d), lambda e, i: (e, i, 0)),
                  pl.BlockSpec((None, tr, 1), lambda e, i: (e, i, 0)),
                  pl.BlockSpec((None, None, d, f), lambda e, i: (layer, e, 0, 0)),
                  pl.BlockSpec((None, None, d, f), lambda e, i: (layer, e, 0, 0)),
                  pl.BlockSpec((None, None, f, d), lambda e, i: (layer, e, 0, 0))],
        out_specs=[pl.BlockSpec((None, tr, d), lambda e, i: (e, i, 0))] * 2,
        out_shape=[jax.ShapeDtypeStruct((ne, r, d), BF16)] * 2,
        scratch_shapes=[pltpu.VMEM((d, f), BF16), pltpu.VMEM((d, f), BF16), pltpu.VMEM((f, d), BF16)],
        compiler_params=_cparams("parallel", "arbitrary"),
        name="moe_ffn",
    )(xe, gs, w1, w3, w2)


def _combine_kernel(off_sm, x1_ref, mod_ref, pos_ref, yh_ref, yl_ref, o_ref, *, cap, nt, w):
    bi, j = pl.program_id(0), pl.program_id(2)
    ne, tt = pos_ref.shape
    rows = lax.broadcasted_iota(I32, (w, tt), 0)
    pos = pos_ref[...]
    wins = _tile_windows(off_sm, bi * ne * (nt + 1) + j, nt + 1, pos, cap, w, rows)
    onehot = _onehots(wins)
    tn = functools.partial(lax.dot_general, dimension_numbers=_TN, preferred_element_type=F32)
    slots = [ref[e, pl.ds(base, w), :] for ref in (yh_ref, yl_ref) for e, (_, base, _) in enumerate(wins)]
    o_ref[...] = tn(jnp.concatenate([onehot, onehot], axis=0), jnp.concatenate(slots, axis=0))

    for e, (p1, base, _) in enumerate(wins):

        @pl.when(p1 > base + w)
        def _():
            def more(lo):
                base2, hit2 = _window_onehot(pos[e:e + 1], lo, cap, w, rows)
                oh = jnp.where(hit2, 1.0, 0.0).astype(BF16)
                o_ref[...] += tn(oh, yh_ref[e, pl.ds(base2, w), :]) + tn(oh, yl_ref[e, pl.ds(base2, w), :])
                return base2 + w

            lax.while_loop(lambda lo: lo < p1, more, base + w)

    o_ref[...] = x1_ref[...] + mod_ref[...][5:6] * o_ref[...]


def _combine(off_flat, x1, mod, stream, pos, ye, t0_blk, t, cap):
    b, s, d = x1.shape
    ne = pos.shape[1]
    tt = min(TOK_TILE, t)
    nt = t // tt
    w = min(SLOT_WINDOW, cap)
    dh = d // COMBINE_SPLIT
    kern = functools.partial(_combine_kernel, cap=cap, nt=nt, w=w)
    grid_spec = pltpu.PrefetchScalarGridSpec(
        num_scalar_prefetch=1,
        grid=(b, COMBINE_SPLIT, nt),
        in_specs=[pl.BlockSpec((None, tt, dh), lambda i, c, j, o: (i, t0_blk * nt + j, c)),
                  pl.BlockSpec((None, None, 6, dh), lambda i, c, j, o: (i, stream, 0, c)),
                  pl.BlockSpec((None, ne, tt), lambda i, c, j, o: (i, 0, j)),
                  pl.BlockSpec((ne, None, cap, dh), lambda i, c, j, o: (0, i, 0, c)),
                  pl.BlockSpec((ne, None, cap, dh), lambda i, c, j, o: (0, i, 0, c))],
        out_specs=pl.BlockSpec((None, tt, dh), lambda i, c, j, o: (i, t0_blk * nt + j, c)))
    return pl.pallas_call(
        kern,
        grid_spec=grid_spec,
        out_shape=jax.ShapeDtypeStruct((b, s, d), F32),
        input_output_aliases={1: 0},
        compiler_params=_cparams("parallel", "parallel", "arbitrary"),
        name="moe_combine",
    )(off_flat, x1, mod, pos, *ye)


def _moe_stream(x1, bm, logits_t, mod, stream, t0_blk, t, experts):
    b = x1.shape[0]
    cap = CAPACITY_FACTOR * t // N_EXPERTS
    pos, aff, offs = _select(logits_t, t0_blk, t, cap)
    nt = t // min(TOK_TILE, t)
    off_flat = offs[:, :, :nt + 1].reshape(-1)
    xe, gs = _gather(off_flat, bm, pos, aff, t0_blk, t, cap)
    d = x1.shape[-1]
    ye = _ffn(xe.reshape(N_EXPERTS, b * cap, d), gs.reshape(N_EXPERTS, b * cap, 1), *experts)
    ye = [y.reshape(N_EXPERTS, b, cap, d) for y in ye]
    return _combine(off_flat, x1, mod, stream, pos, ye, t0_blk, t, cap)


def _rwkv_stream_kernel(*refs, t_lat, has_vres):
    if has_vres:
        (x_ref, xp_ref, xn_ref, mod_ref, g1_ref, mu_ref, wr_ref, wk_ref, wv_ref, w1_ref, w2_ref, w0_ref,
         a1_ref, a2_ref, a0_ref, gg1_ref, gg2_ref, kk_ref, ka_ref, rk_ref, seg_ref, segt_ref,
         vf_ref, v0_ref, v1_ref, v2_ref,
         r_out, v_out, kk_out, lw_out, kd_out, b_out, g_out, bonus_out) = refs
    else:
        (x_ref, xp_ref, xn_ref, mod_ref, g1_ref, mu_ref, wr_ref, wk_ref, wv_ref, w1_ref, w2_ref, w0_ref,
         a1_ref, a2_ref, a0_ref, gg1_ref, gg2_ref, kk_ref, ka_ref, rk_ref, seg_ref, segt_ref,
         r_out, v_out, kk_out, lw_out, kd_out, b_out, g_out, bonus_out) = refs
    j = pl.program_id(1)
    mod = mod_ref[...]
    g1 = g1_ref[...]
    nm = lambda x: _norm_mod(x, g1, mod[1:2], mod[0:1])
    h = nm(x_ref[...])
    tm, d = h.shape
    hp = nm(xp_ref[...])[SUBLANE - 1:SUBLANE]
    hn = nm(xn_ref[...])[0:1]
    row = lax.broadcasted_iota(I32, (tm, 1), 0)
    grow = row + j * tm
    s_tot = pl.num_programs(1) * tm
    first = (grow == 0) | (grow == t_lat)
    last = (grow == t_lat - 1) | (grow == s_tot - 1)
    prev = jnp.where(row == 0, hp, pltpu.roll(h, 1, 0))
    prev = jnp.where(first, 0.0, prev)
    nxt = jnp.where(row == tm - 1, hn, pltpu.roll(h, tm - 1, 0))
    nxt = jnp.where(last, 0.0, nxt)
    xx = 0.5 * (prev + nxt) - h
    mu = mu_ref[...]
    xs = [h + xx * mu[m:m + 1] for m in range(6)]
    r = _bdot(xs[0], wr_ref[...])
    k = _bdot(xs[1], wk_ref[...])
    v = _bdot(xs[2], wv_ref[...])
    if has_vres:
        gate = _sigmoid(v0_ref[...] + _bdot(_bdot(xs[2], v1_ref[...]), v2_ref[...]))
        v = v + (vf_ref[...] - v) * gate
    wz = w0_ref[...] + _bdot(jnp.tanh(_bdot(xs[3], w1_ref[...])), w2_ref[...])
    lw = -math.exp(-0.5) * _sigmoid(wz)
    a = _sigmoid(a0_ref[...] + _bdot(_bdot(xs[4], a1_ref[...]), a2_ref[...]))
    g = _bdot(_sigmoid(_bdot(xs[5], gg1_ref[...])), gg2_ref[...])
    seg, segt = seg_ref[...], segt_ref[...]
    segsum = lambda t: _dot_exact_rhs01(_dot_exact_rhs01(t, seg), segt)
    kkr = k * kk_ref[...]
    kk = kkr / jnp.maximum(jnp.sqrt(segsum(kkr * kkr)), 1e-12)
    ka = ka_ref[...]
    kd0 = k * (1.0 + (a[:, :d] - 1.0) * ka)
    kd1 = k * (1.0 + (a[:, d:] - 1.0) * ka)
    r_out[...] = r
    v_out[...] = v
    kk_out[...] = kk
    lw_out[...] = lw
    kd_out[:, :d] = kd0
    kd_out[:, d:] = kd1
    b_out[:, :d] = kk * a[:, :d]
    b_out[:, d:] = kk * a[:, d:]
    g_out[...] = g
    bonus_out[...] = segsum(r * (kd0 + kd1) * rk_ref[...]) * v


def _rwkv_stream(x, mod, g1, wts, vres, t_lat):
    b, s, d = x.shape
    tm = TOK_TILE
    nlt = t_lat // tm
    nsub = tm // SUBLANE
    const = lambda a: pl.BlockSpec(a.shape, lambda i, j: (0,) * a.ndim)
    tok = lambda w: pl.BlockSpec((None, tm, w), lambda i, j: (i, j, 0))
    in_specs = [tok(d),
                pl.BlockSpec((None, SUBLANE, d), lambda i, j: (i, jnp.maximum(j * nsub - 1, 0), 0)),
                pl.BlockSpec((None, SUBLANE, d), lambda i, j: (i, jnp.minimum((j + 1) * nsub, s // SUBLANE - 1), 0)),
                pl.BlockSpec((None, None, 6, d), lambda i, j: (i, (j >= nlt).astype(I32), 0, 0)),
                const(g1)] + [const(a) for a in wts]
    args = [x, x, x, mod, g1] + list(wts)
    if vres is not None:
        vf, v0, v1, v2 = vres
        in_specs += [tok(d), const(v0), const(v1), const(v2)]
        args += [vf, v0, v1, v2]
    kern = functools.partial(_rwkv_stream_kernel, t_lat=t_lat, has_vres=vres is not None)
    widths = [d, d, d, 2 * d, 2 * d, 2 * d, d, d]
    return pl.pallas_call(
        kern,
        grid=(b, s // tm),
        in_specs=in_specs,
        out_specs=[tok(w) for w in widths],
        out_shape=[jax.ShapeDtypeStruct((b, s, w), F32) for w in widths],
        compiler_params=_cparams("parallel", "parallel"),
        name="rwkv_stream",
    )(*args)


def _scan_kernel(r_ref, v_ref, kk_ref, lw_ref, kd_ref, b_ref, y_ref, h_ref, *, reverse, nchunk):
    L = SCAN_CHUNK

    @pl.when(pl.program_id(2) == 0)
    def _():
        h_ref[...] = jnp.zeros_like(h_ref)

    n2 = 2 * L
    rr = lax.broadcasted_iota(I32, (n2, n2), 0)
    cc = lax.broadcasted_iota(I32, (n2, n2), 1)
    tr, tc = rr % L, cc % L
    same = (rr // L) == (cc // L)
    before = (tr < tc) if reverse else (tr > tc)
    strict = same & before
    incl = same & (before | (tr == tc))
    eye = rr == cc
    blk16 = (rr // 16) == (cc // 16)
    blk32 = (rr // 32) == (cc // 32)
    ri = lax.broadcasted_iota(I32, (L, L), 0)
    ci = lax.broadcasted_iota(I32, (L, L), 1)
    tri = jnp.where((ci >= ri) if reverse else (ci <= ri), 1.0, 0.0).astype(BF16)
    head0 = lax.broadcasted_iota(I32, (L, LANE), 1) < RW_HEAD

    def stack(x):
        return jnp.concatenate([jnp.where(head0, x, 0.0), jnp.where(head0, 0.0, x)], axis=0)

    order = list(range(nchunk - 1, -1, -1) if reverse else range(nchunk))
    npp = y_ref.shape[-1] // LANE
    units = [(pp, ch) for pp in range(npp) for ch in order]

    def each(f, *lists):
        return [f(*a) for a in zip(*lists)]

    def ld(ref):
        return [ref[pl.ds(ch * L, L), pl.ds(pp * LANE, LANE)] for pp, ch in units]

    dd = functools.partial(jnp.dot, preferred_element_type=F32)
    lw = ld(lw_ref)
    lws = each(_split3, lw)
    cs = each(lambda t: dd(tri, t[0]) + (dd(tri, t[1]) + dd(tri, t[2])), lws)
    total = each(lambda c: c[0:1] if reverse else c[L - 1:L], cs)
    gam = each(jnp.exp, cs)
    ginv = each(lambda c: jnp.exp(-c), cs)
    gprev = each(lambda c, l: jnp.exp(c - l), cs, lw)
    gend = each(lambda t, c: jnp.exp(t - c), total, cs)
    kk, bb, kd, r, v = ld(kk_ref), ld(b_ref), ld(kd_ref), ld(r_ref), ld(v_ref)
    mul_stack = lambda a, g: stack(a * g)
    rtm = each(mul_stack, r, gam)
    rts = each(_split2, rtm)
    kkm, btm, ktm = (each(_split2, each(mul_stack, a, g)) for a, g in ((kk, gprev), (bb, ginv), (kd, ginv)))
    bht, kht = (each(lambda a, g: _split2(stack(a * g).T), a, gend) for a in (bb, kd))
    vm = each(_split2, each(stack, v))
    n_ab = each(lambda a, c, e: _mm3_rows([a, c], e, _NT), kkm, rts, btm)
    m_ak = each(lambda a, c, e: _mm3_rows([a, c], e, _NT), kkm, rts, ktm)
    n = each(lambda t: jnp.where(strict, t[0], 0.0), n_ab)
    ab = each(lambda t: jnp.where(incl, t[1], 0.0), n_ab)
    m = each(lambda t: jnp.where(strict, t[0], 0.0), m_ak)
    ak = each(lambda t: jnp.where(incl, t[1], 0.0), m_ak)
    nd = each(lambda t: jnp.where(blk16, t, 0.0), n)
    x = each(lambda t: jnp.where(eye, 1.0, 0.0) - t, nd)
    pw = nd
    for _ in range(3):
        pw = each(lambda t: _bdot(t, t), pw)
        x = each(lambda a, c: a + _bdot(a, c), x, pw)
    ident = jnp.where(eye, 1.0, 0.0)
    for inner, outer in ((blk16, blk32), (blk32, same)):
        c = each(lambda t: jnp.where(outer & jnp.logical_not(inner), t, 0.0), n)
        e = each(lambda a: a - ident, x)
        ec = each(_bdot, e, c)
        x = each(lambda a, t, u, s: a - t - (u + _bdot(t + u, s)), x, c, ec, e)
    xs = each(_split2, x)
    abs_ = each(_split2, ab)
    mv_akv_khv = each(lambda a, c, e, f: _mm3_rows([_split2(a), _split2(c), e], f), m, ak, kht, vm)
    cat = lambda a, c: tuple(jnp.concatenate([s, t], axis=1) for s, t in zip(a, c))
    wu = each(lambda a, c, t: _split2(_mm3(a, cat(c, _split2(t[0])))), xs, kkm, mv_akv_khv)
    bw_abw = each(lambda a, c, e: _mm3_rows([a, c], e), bht, abs_, wu)
    p = each(lambda t, a: jnp.where(eye, jnp.exp(t), 0.0) - a[0][:, :LANE], total, bw_abw)
    q = each(lambda t, a: t[2] - a[0][:, LANE:], mv_akv_khv, bw_abw)
    rres = each(lambda t, a: t - a[1][:, :LANE], rtm, bw_abw)
    y0 = each(lambda t, a: t[1] - a[1][:, LANE:], mv_akv_khv, bw_abw)
    for i, (pp, ch) in enumerate(units):
        y, hn = _mm3_rows([_split2(rres[i]), _split2(p[i])], _split2(h_ref[pp]))
        y = y + y0[i]
        y_ref[pl.ds(ch * L, L), pl.ds(pp * LANE, LANE)] = y[:L] + y[L:]
        h_ref[pp] = hn + q[i]


def _scan(r, v, kk, lw, kd, bb, t_lat, reverse):
    b, s, d = r.shape
    tb = TOK_TILE
    nl, nc = t_lat // tb, (s - t_lat) // tb
    z = 1 if reverse else 0

    def blk(c):
        if reverse:
            return jnp.where(c < nc, nl + nc - 1 - c, nl - 1 - (c - nc))
        return jnp.where(c < nc, nl + c, c - nc)

    wl = SCAN_PAIRS * LANE
    ngrp = d // wl
    shared = pl.BlockSpec((None, tb, wl), lambda i, p, c: (i, blk(c), p))
    dirn = pl.BlockSpec((None, tb, wl), lambda i, p, c: (i, blk(c), z * ngrp + p))
    kern = functools.partial(_scan_kernel, reverse=reverse, nchunk=tb // SCAN_CHUNK)
    return pl.pallas_call(
        kern,
        grid=(b, ngrp, nl + nc),
        in_specs=[shared, shared, shared, dirn, dirn, dirn],
        out_specs=pl.BlockSpec((None, tb, wl), lambda i, p, c: (i, blk(c), p)),
        out_shape=jax.ShapeDtypeStruct((b, s, d), F32),
        scratch_shapes=[pltpu.VMEM((SCAN_PAIRS, LANE, LANE), F32)],
        compiler_params=_cparams("parallel", "parallel", "arbitrary"),
        name="wkv_scan_bwd" if reverse else "wkv_scan_fwd",
    )(r, v, kk, lw, kd, bb)


def _rwkv_out_kernel(yf_ref, yb_ref, bonus_ref, g_ref, lg_ref, lb_ref, seg_ref, segt_ref, o_ref):
    seg, segt = seg_ref[...], segt_ref[...]
    segmean = lambda t: _dot_exact_rhs01(_dot_exact_rhs01(t, seg), segt) * (1.0 / RW_HEAD)
    y = yf_ref[...] + yb_ref[...]
    dlt = y - segmean(y)
    var = segmean(dlt * dlt)
    yn = (dlt * lax.rsqrt(var + RW_LNX_EPS)) * lg_ref[...] + lb_ref[...]
    o_ref[...] = ((yn + bonus_ref[...]) * g_ref[...]).astype(BF16)


def _rwkv_out(yf, yb, bonus, g, lnx_g, lnx_b, seg, segt):
    b, s, d = yf.shape
    tm = TOK_TILE
    tok = pl.BlockSpec((None, tm, d), lambda i, j: (i, j, 0))
    const = lambda a: pl.BlockSpec(a.shape, lambda i, j: (0,) * a.ndim)
    return pl.pallas_call(
        _rwkv_out_kernel,
        grid=(b, s // tm),
        in_specs=[tok, tok, tok, tok, const(lnx_g), const(lnx_b), const(seg), const(segt)],
        out_specs=tok,
        out_shape=jax.ShapeDtypeStruct((b, s, d), BF16),
        compiler_params=_cparams("parallel", "parallel"),
        name="rwkv_out",
    )(yf, yb, bonus, g, lnx_g, lnx_b, seg, segt)


def _final_kernel(x_ref, g_ref, o_ref):
    o_ref[...] = _rms(x_ref[...], g_ref[...])


def _final_norm(x, g, t_lat):
    b, _, d = x.shape
    tm = TOK_TILE
    return pl.pallas_call(
        _final_kernel,
        grid=(b, t_lat // tm),
        in_specs=[pl.BlockSpec((None, tm, d), lambda i, j: (i, j, 0)),
                  pl.BlockSpec(g.shape, lambda i, j: (0, 0))],
        out_specs=pl.BlockSpec((None, tm, d), lambda i, j: (i, j, 0)),
        out_shape=jax.ShapeDtypeStruct((b, t_lat, d), F32),
        compiler_params=_cparams("parallel", "parallel"),
        name="final_norm",
    )(x, g)


def _rope_tables(t_lat, t_ctx):
    rows = t_lat // GRID_W
    r = jnp.broadcast_to(jnp.arange(rows)[:, None], (rows, GRID_W)).reshape(-1).astype(F32)
    col = jnp.broadcast_to(jnp.arange(GRID_W)[None, :], (rows, GRID_W)).reshape(-1).astype(F32)
    n_freq = MLA_ROPE // 4
    inv = ROPE_THETA ** (-jnp.arange(n_freq, dtype=F32) / n_freq)
    ang = jnp.concatenate([r[:, None] * inv, col[:, None] * inv], axis=-1)
    cos, sin = jnp.cos(ang), jnp.sin(ang)
    cosf = jnp.concatenate([cos, cos, cos, cos], axis=-1)
    sins = jnp.concatenate([-sin, sin, -sin, sin], axis=-1)
    cosf = jnp.concatenate([cosf, jnp.ones((t_ctx, LANE), F32)], axis=0)
    sins = jnp.concatenate([sins, jnp.zeros((t_ctx, LANE), F32)], axis=0)
    return cosf, sins


def _blockdiag2(w):
    z = jnp.zeros_like(w[0])
    return jnp.concatenate([jnp.concatenate([w[0], z], axis=1), jnp.concatenate([z, w[1]], axis=1)], axis=0)


def kernel(x, c, ctx, c_ctx, mod_w, mod_b, norm1_g, norm2_g, final_g, mla_w_in, mla_g_q, mla_g_kv, mla_w_uq, mla_w_ukv, mla_w_o, rw_mu, rw_w_rkv, rw_w0, rw_w1, rw_w2, rw_a0, rw_a1, rw_a2, rw_v0, rw_v1, rw_v2, rw_g1, rw_g2, rw_k_k, rw_k_a, rw_r_k, rw_lnx_g, rw_lnx_b, rw_w_o, moe_router, moe_w1, moe_w3, moe_w2):
    b, t_lat, d = x.shape
    t_ctx = ctx.shape[1]
    depth = mod_w.shape[0]
    assert b + 1 <= SUBLANE and t_lat % TOK_TILE == 0 and t_ctx % TOK_TILE == 0 and d % LANE == 0
    n_lat_tiles = t_lat // TOK_TILE
    ctx_blk = t_lat // t_ctx
    assert ctx_blk * t_ctx == t_lat

    cc = jnp.concatenate([c, c_ctx[None], jnp.zeros((SUBLANE - b - 1, d), F32)], axis=0)
    mods = _modulation(cc, mod_w, mod_b).reshape(depth, SUBLANE, 6, d)
    mod_all = jnp.stack([mods[:, :b], jnp.broadcast_to(mods[:, b:b + 1], (depth, b, 6, d))], axis=2)

    xs = jnp.concatenate([x, ctx], axis=1)
    cosf, sins = _rope_tables(t_lat, t_ctx)
    row = lambda a: a.reshape(1, -1)
    seg = (jnp.arange(d)[:, None] // RW_HEAD == jnp.arange(LANE)[None, :]).astype(BF16)
    segt = seg.T

    v_first = None
    for layer in range(depth):
        need_ctx = layer < depth - 1
        mod = mod_all[layer]
        j = layer // 2
        if layer % 2 == 0:
            win = jnp.pad(mla_w_in[j], ((0, 0), (0, 4 * LANE - mla_w_in.shape[-1]))).astype(BF16)
            wuq = mla_w_uq[j].reshape(MLA_Q_LORA, MLA_HEADS, MLA_NOPE + MLA_ROPE)
            wuq = jnp.concatenate([wuq[..., :MLA_NOPE].reshape(MLA_Q_LORA, -1),
                                   wuq[..., MLA_NOPE:].reshape(MLA_Q_LORA, -1)], axis=1).astype(BF16)
            q, k, v = _mla_proj(xs, mod, row(norm1_g[layer]), win, row(mla_g_q[j]), row(mla_g_kv[j]),
                                wuq, mla_w_ukv[j].astype(BF16), cosf, sins, n_lat_tiles)
            y = _attention(q, k, v, t_lat)
            wo = mla_w_o[j].astype(BF16)
        else:
            lora = lambda w: jnp.concatenate([w[0], w[1]], axis=1).astype(BF16)
            wts = [rw_mu[j], rw_w_rkv[j, 0].astype(BF16), rw_w_rkv[j, 1].astype(BF16), rw_w_rkv[j, 2].astype(BF16),
                   lora(rw_w1[j]), _blockdiag2(rw_w2[j]).astype(BF16), rw_w0[j].reshape(1, -1),
                   lora(rw_a1[j]), _blockdiag2(rw_a2[j]).astype(BF16), rw_a0[j].reshape(1, -1),
                   rw_g1[j].astype(BF16), rw_g2[j].astype(BF16),
                   row(rw_k_k[j]), row(rw_k_a[j]), row(rw_r_k[j]), seg, segt]
            vres = None
            if j > 0:
                pad = LANE - rw_v1.shape[-1]
                vres = (v_first, row(rw_v0[j - 1]), jnp.pad(rw_v1[j - 1], ((0, 0), (0, pad))).astype(BF16),
                        jnp.pad(rw_v2[j - 1], ((0, pad), (0, 0))).astype(BF16))
            r, v, kk, lw, kd, bb, g, bonus = _rwkv_stream(xs, mod, row(norm1_g[layer]), wts, vres, t_lat)
            if j == 0:
                v_first = v
            yf = _scan(r, v, kk, lw, kd, bb, t_lat, reverse=False)
            yb = _scan(r, v, kk, lw, kd, bb, t_lat, reverse=True)
            y = _rwkv_out(yf, yb, bonus, g, row(rw_lnx_g[j]), row(rw_lnx_b[j]), seg, segt)
            wo = rw_w_o[j].astype(BF16)
        x1, bm, logits_t = _post_mixer(xs, y, wo, mod, row(norm2_g[layer]), moe_router[layer].T.astype(BF16),
                                       n_lat_tiles)
        experts = (moe_w1, moe_w3, moe_w2, layer)
        xs = _moe_stream(x1, bm, logits_t, mod, 0, 0, t_lat, experts)
        if need_ctx:
            xs = _moe_stream(xs, bm, logits_t, mod, 1, ctx_blk, t_ctx, experts)
    return _final_norm(xs, row(final_g), t_lat)
```

```python
import functools
import math

import jax
import jax.numpy as jnp
from jax import lax
from jax.experimental import pallas as pl
from jax.experimental.pallas import tpu as pltpu

F32, BF16, I32 = jnp.float32, jnp.bfloat16, jnp.int32

GRID_W = 64
MLA_HEADS, MLA_NOPE, MLA_ROPE, MLA_V = 8, 128, 64, 128
MLA_Q_LORA, MLA_KV_LORA = 256, 128
MLA_SCALE = (MLA_NOPE + MLA_ROPE) ** -0.5
ROPE_THETA = 10000.0
RW_HEAD = 64
RW_LNX_EPS = 64e-5
N_EXPERTS = 16
CAPACITY_FACTOR = 2
RMS_EPS = 1e-6

LANE = 128
SUBLANE = 8
TOK_TILE = 256
WIN_ALIGN = 16
SLOT_WINDOW = 64
COMBINE_SPLIT = 2
ATTN_TILE = 1024
ATTN_ROWS = 256
SCAN_CHUNK = 64
SCAN_PAIRS = 4
VMEM_LIMIT = 56 * 1024 * 1024


def _cparams(*sem):
    return pltpu.CompilerParams(dimension_semantics=sem, vmem_limit_bytes=VMEM_LIMIT)


def _bdot(a, b):
    return jnp.dot(a.astype(BF16), b.astype(BF16), preferred_element_type=F32)


_NT = (((1,), (1,)), ((), ()))
_TN = (((0,), (0,)), ((), ()))
_NN = (((1,), (0,)), ((), ()))


def _split2(x):
    hi = x.astype(BF16)
    lo = (x - hi.astype(F32)).astype(BF16)
    return hi, lo


def _split3(x):
    hi = x.astype(BF16)
    r = x - hi.astype(F32)
    mid = r.astype(BF16)
    lo = (r - mid.astype(F32)).astype(BF16)
    return hi, mid, lo


def _mm3(a, b, dims=_NN):
    (lc,), (rc,) = dims[0]
    lhs = jnp.concatenate([a[0], a[1], a[0]], axis=lc)
    rhs = jnp.concatenate([b[0], b[0], b[1]], axis=rc)
    return lax.dot_general(lhs, rhs, dims, preferred_element_type=F32)


def _mm3_rows(lhs, b, dims=_NN):
    his = jnp.concatenate([a[0] for a in lhs], axis=0)
    los = jnp.concatenate([a[1] for a in lhs], axis=0)
    res = _mm3((his, los), b, dims)
    outs, o = [], 0
    for a in lhs:
        outs.append(res[o:o + a[0].shape[0]])
        o += a[0].shape[0]
    return outs


def _dot_exact_rhs01(x, m01):
    h, l = _split2(x)
    return jnp.dot(jnp.concatenate([h, l], axis=1), jnp.concatenate([m01, m01], axis=0),
                   preferred_element_type=F32)


def _rms(x, g):
    return (x * lax.rsqrt(jnp.mean(x * x, axis=-1, keepdims=True) + RMS_EPS)) * g


def _norm_mod(x, g, scale, shift):
    return _rms(x, g) * (1.0 + scale) + shift


def _sigmoid(x):
    return 1.0 / (1.0 + jnp.exp(-x))


def _mod_kernel(c_ref, w_ref, b_ref, o_ref):
    s = c_ref[...]
    s = s * _sigmoid(s)
    o_ref[...] = _bdot(s, w_ref[...]) + b_ref[...]


def _modulation(cc, mod_w, mod_b):
    depth, d, n = mod_w.shape
    tn = n // 4
    return pl.pallas_call(
        _mod_kernel,
        grid=(depth, n // tn),
        in_specs=[pl.BlockSpec((SUBLANE, d), lambda l, j: (0, 0)),
                  pl.BlockSpec((None, d, tn), lambda l, j: (l, 0, j)),
                  pl.BlockSpec((None, 1, tn), lambda l, j: (l, 0, j))],
        out_specs=pl.BlockSpec((None, SUBLANE, tn), lambda l, j: (l, 0, j)),
        out_shape=jax.ShapeDtypeStruct((depth, SUBLANE, n), F32),
        compiler_params=_cparams("arbitrary", "arbitrary"),
        name="modulation",
    )(cc, mod_w, mod_b.reshape(depth, 1, n))


def _rope128(x, cosf, sins):
    lane = lax.broadcasted_iota(I32, x.shape, 1)
    first = (lane % MLA_ROPE) < (MLA_ROPE // 2)
    rot = jnp.where(first, pltpu.roll(x, LANE - MLA_ROPE // 2, 1), pltpu.roll(x, MLA_ROPE // 2, 1))
    return x * cosf + rot * sins


def _mla_proj_kernel(x_ref, mod_ref, g1_ref, win_ref, gq_ref, gkv_ref, wuq_ref, wukv_ref,
                     cos_ref, sin_ref, q_ref, k_ref, v_ref):
    mod = mod_ref[...]
    h = _norm_mod(x_ref[...], g1_ref[...], mod[1:2], mod[0:1])
    z = _bdot(h, win_ref[...])
    cq = _rms(z[:, :MLA_Q_LORA], gq_ref[...])
    ckv = _rms(z[:, MLA_Q_LORA:MLA_Q_LORA + MLA_KV_LORA], gkv_ref[...])
    kr = z[:, MLA_Q_LORA + MLA_KV_LORA:]
    q = _bdot(cq, wuq_ref[...])
    kv = _bdot(ckv, wukv_ref[...])
    cosf, sins = cos_ref[...], sin_ref[...]
    kr_lo = _rope128(kr, cosf, sins)
    kr_hi = pltpu.roll(kr_lo, MLA_ROPE, 1)
    nope_w = MLA_HEADS * MLA_NOPE
    for g in range(MLA_HEADS // 2):
        qr = _rope128(q[:, nope_w + LANE * g:nope_w + LANE * (g + 1)], cosf, sins)
        for hh in (2 * g, 2 * g + 1):
            qn = q[:, MLA_NOPE * hh:MLA_NOPE * (hh + 1)]
            q_ref[hh] = jnp.concatenate([qn, qr], axis=1).astype(BF16)
    for hh in range(MLA_HEADS):
        base = (MLA_NOPE + MLA_V) * hh
        krh = kr_lo if hh % 2 == 0 else kr_hi
        k_ref[hh] = jnp.concatenate([kv[:, base:base + MLA_NOPE], krh], axis=1).astype(BF16)
        v_ref[hh] = kv[:, base + MLA_NOPE:base + MLA_NOPE + MLA_V].astype(BF16)


def _mla_proj(x, mod, g1, win, gq, gkv, wuq, wukv, cosf, sins, n_lat_tiles):
    b, s, d = x.shape
    tm = TOK_TILE
    const = lambda shape: pl.BlockSpec(shape, lambda i, j: (0,) * len(shape))
    hspec = lambda w: pl.BlockSpec((None, MLA_HEADS, tm, w), lambda i, j: (i, 0, j, 0))
    return pl.pallas_call(
        _mla_proj_kernel,
        grid=(b, s // tm),
        in_specs=[pl.BlockSpec((None, tm, d), lambda i, j: (i, j, 0)),
                  pl.BlockSpec((None, None, 6, d), lambda i, j: (i, (j >= n_lat_tiles).astype(I32), 0, 0)),
                  const(g1.shape), const(win.shape), const(gq.shape), const(gkv.shape),
                  const(wuq.shape), const(wukv.shape),
                  pl.BlockSpec((tm, LANE), lambda i, j: (j, 0)),
                  pl.BlockSpec((tm, LANE), lambda i, j: (j, 0))],
        out_specs=[hspec(2 * LANE), hspec(2 * LANE), hspec(MLA_V)],
        out_shape=[jax.ShapeDtypeStruct((b, MLA_HEADS, s, 2 * LANE), BF16),
                   jax.ShapeDtypeStruct((b, MLA_HEADS, s, 2 * LANE), BF16),
                   jax.ShapeDtypeStruct((b, MLA_HEADS, s, MLA_V), BF16)],
        compiler_params=_cparams("parallel", "parallel"),
        name="mla_proj",
    )(x, mod, g1, win, gq, gkv, wuq, wukv, cosf, sins)


def _attn_kernel(q_ref, k_ref, v_ref, o_ref, *, n_lat_tiles, t_lat):
    j = pl.program_id(2)
    t_ctx = k_ref.shape[0] - t_lat

    def softmax_pv(s, v):
        m = jnp.max(s, axis=-1, keepdims=True)
        p = jnp.exp2((s - m) * (MLA_SCALE * math.log2(math.e)))
        l = jnp.sum(p, axis=-1, keepdims=True)
        o = jnp.dot(p.astype(BF16), v, preferred_element_type=F32)
        return (o / l).astype(o_ref.dtype)

    @pl.when(j < n_lat_tiles)
    def _():
        for r0 in range(0, o_ref.shape[0], ATTN_ROWS):
            s = lax.dot_general(q_ref[r0:r0 + ATTN_ROWS, :], k_ref[...], _NT, preferred_element_type=F32)
            o_ref[r0:r0 + ATTN_ROWS, :] = softmax_pv(s, v_ref[...])

    @pl.when(j >= n_lat_tiles)
    def _():
        s = lax.dot_general(q_ref[:t_ctx, :], k_ref[t_lat:, :], _NT, preferred_element_type=F32)
        o_ref[:t_ctx, :] = softmax_pv(s, v_ref[t_lat:, :])


def _attention(q, k, v, t_lat):
    b, nh, s, _ = q.shape
    tq = min(ATTN_TILE, t_lat)
    assert t_lat % tq == 0 and s - t_lat <= tq
    kern = functools.partial(_attn_kernel, n_lat_tiles=t_lat // tq, t_lat=t_lat)
    return pl.pallas_call(
        kern,
        grid=(b, nh, pl.cdiv(s, tq)),
        in_specs=[pl.BlockSpec((None, None, tq, q.shape[-1]), lambda i, h, j: (i, h, j, 0)),
                  pl.BlockSpec((None, None, s, k.shape[-1]), lambda i, h, j: (i, h, 0, 0)),
                  pl.BlockSpec((None, None, s, v.shape[-1]), lambda i, h, j: (i, h, 0, 0))],
        out_specs=pl.BlockSpec((None, tq, MLA_V), lambda i, h, j: (i, j, h)),
        out_shape=jax.ShapeDtypeStruct((b, s, nh * MLA_V), BF16),
        compiler_params=_cparams("parallel", "parallel", "arbitrary"),
        name="mla_attention",
    )(q, k, v)


def _post_mixer_kernel(*refs, n_mix):
    x_ref, mix = refs[0], refs[1:1 + n_mix]
    wo_ref, mod_ref, g2_ref, rt_ref, x1_ref, bm_ref, lg_ref = refs[1 + n_mix:]
    if n_mix == 1:
        y = mix[0][...]
    else:
        yf_ref, yb_ref, bonus_ref, g_ref, lng_ref, lnb_ref, seg_ref, segt_ref = mix
        seg, segt = seg_ref[...], segt_ref[...]
        segmean = lambda t: _dot_exact_rhs01(_dot_exact_rhs01(t, seg), segt) * (1.0 / RW_HEAD)
        yy = yf_ref[...] + yb_ref[...]
        dlt = yy - segmean(yy)
        var = segmean(dlt * dlt)
        yn = (dlt * lax.rsqrt(var + RW_LNX_EPS)) * lng_ref[...] + lnb_ref[...]
        y = ((yn + bonus_ref[...]) * g_ref[...]).astype(BF16)
    mod = mod_ref[...]
    x1 = x_ref[...] + mod[2:3] * jnp.dot(y, wo_ref[...], preferred_element_type=F32)
    x1_ref[...] = x1
    bm = _norm_mod(x1, g2_ref[...], mod[4:5], mod[3:4]).astype(BF16)
    bm_ref[...] = bm
    lg_ref[...] = lax.dot_general(rt_ref[...], bm, _NT, preferred_element_type=F32)


def _post_mixer(x, mix, wo, mod, g2, router_t, n_lat_tiles):
    b, s, d = x.shape
    tm = TOK_TILE
    const = lambda shape: pl.BlockSpec(shape, lambda i, j: (0,) * len(shape))
    mix_specs = [pl.BlockSpec((None, tm, a.shape[-1]), lambda i, j: (i, j, 0)) if a.ndim == 3 else const(a.shape)
                 for a in mix]
    return pl.pallas_call(
        functools.partial(_post_mixer_kernel, n_mix=len(mix)),
        grid=(b, s // tm),
        in_specs=[pl.BlockSpec((None, tm, d), lambda i, j: (i, j, 0))] + mix_specs + [
                  const(wo.shape),
                  pl.BlockSpec((None, None, 6, d), lambda i, j: (i, (j >= n_lat_tiles).astype(I32), 0, 0)),
                  const(g2.shape), const(router_t.shape)],
        out_specs=[pl.BlockSpec((None, tm, d), lambda i, j: (i, j, 0)),
                   pl.BlockSpec((None, tm, d), lambda i, j: (i, j, 0)),
                   pl.BlockSpec((None, N_EXPERTS, tm), lambda i, j: (i, 0, j))],
        out_shape=[jax.ShapeDtypeStruct((b, s, d), F32),
                   jax.ShapeDtypeStruct((b, s, d), BF16),
                   jax.ShapeDtypeStruct((b, N_EXPERTS, s), F32)],
        compiler_params=_cparams("parallel", "parallel"),
        name="post_mixer",
    )(x, *mix, wo, mod, g2, router_t)


def _select_kernel(lg_ref, pos_ref, aff_ref, off_ref, *, cap, tt):
    lg = lg_ref[...]
    ne, t = lg.shape
    m = jnp.max(lg, axis=0, keepdims=True)
    ex = jnp.exp(lg - m)
    aff = ex / jnp.sum(ex, axis=0, keepdims=True)
    aff_ref[...] = aff
    bits = pltpu.bitcast(aff, I32)

    def search(i, prefix):
        cand = prefix | jnp.left_shift(jnp.int32(1), 30 - i)
        cnt = jnp.sum(jnp.where(bits >= cand, 1.0, 0.0), axis=1, keepdims=True)
        return jnp.where(cnt >= cap, cand, prefix)

    thr = lax.fori_loop(0, 31, search, jnp.zeros((ne, 1), I32))
    gt = bits > thr
    eq = bits == thr
    need = cap - jnp.sum(jnp.where(gt, 1.0, 0.0), axis=1, keepdims=True)

    nch = t // LANE
    tri = (lax.broadcasted_iota(I32, (LANE, LANE), 0) <= lax.broadcasted_iota(I32, (LANE, LANE), 1))
    tri = jnp.where(tri, 1.0, 0.0).astype(BF16)

    def chunk(a, c):
        return a[:, c * LANE:(c + 1) * LANE]

    off = jnp.zeros((ne, 1), F32)
    sel = []
    for c in range(nch):
        eqc = jnp.where(chunk(eq, c), 1.0, 0.0)
        inc = jnp.dot(eqc.astype(BF16), tri, preferred_element_type=F32)
        rank = inc - eqc + off
        off = off + inc[:, LANE - 1:LANE]
        sel.append(jnp.where(chunk(gt, c) | (chunk(eq, c) & (rank < need)), 1.0, 0.0))

    lane = lax.broadcasted_iota(I32, (ne, LANE), 1)
    offs = jnp.zeros((ne, LANE), F32)
    off = jnp.zeros((ne, 1), F32)
    per_tile = tt // LANE
    for c in range(nch):
        if c % per_tile == 0:
            offs = jnp.where(lane == c // per_tile, off, offs)
        inc = jnp.dot(sel[c].astype(BF16), tri, preferred_element_type=F32)
        pos = inc - sel[c] + off
        off = off + inc[:, LANE - 1:LANE]
        pos_ref[:, c * LANE:(c + 1) * LANE] = jnp.where(sel[c] > 0.0, pos, -1.0).astype(I32)
    offs = jnp.where(lane == nch // per_tile, off, offs)
    off_ref[...] = offs.astype(I32)


def _select(logits_t, t0_blk, t, cap):
    b, ne, _ = logits_t.shape
    tt = min(TOK_TILE, t)
    kern = functools.partial(_select_kernel, cap=cap, tt=tt)
    return pl.pallas_call(
        kern,
        grid=(b,),
        in_specs=[pl.BlockSpec((None, ne, t), lambda i: (i, 0, t0_blk))],
        out_specs=[pl.BlockSpec((None, ne, t), lambda i: (i, 0, 0)),
                   pl.BlockSpec((None, ne, t), lambda i: (i, 0, 0)),
                   pl.BlockSpec((None, ne, LANE), lambda i: (i, 0, 0))],
        out_shape=[jax.ShapeDtypeStruct((b, ne, t), I32),
                   jax.ShapeDtypeStruct((b, ne, t), F32),
                   jax.ShapeDtypeStruct((b, ne, LANE), I32)],
        compiler_params=_cparams("parallel"),
        name="moe_select",
    )(logits_t)


def _window_onehot(posr, lo, cap, w, rows):
    base = pl.multiple_of(jnp.minimum((lo // WIN_ALIGN) * WIN_ALIGN, cap - w), WIN_ALIGN)
    hit = ((posr - base) == rows) & (posr >= lo)
    return base, hit


def _tile_windows(off_sm, obase, stride, pos, cap, w, rows):
    out = []
    for e in range(pos.shape[0]):
        p0, p1 = off_sm[obase + e * stride], off_sm[obase + e * stride + 1]
        base, hit = _window_onehot(pos[e:e + 1], p0, cap, w, rows)
        out.append((p1, base, hit))
    return out


def _onehots(wins):
    return jnp.concatenate([jnp.where(hit, 1.0, 0.0).astype(BF16) for _, _, hit in wins], axis=0)


def _gather_kernel(off_sm, h_ref, pos_ref, aff_ref, xe_ref, gs_ref, *, cap, nt, w):
    bi, j = pl.program_id(0), pl.program_id(1)
    ne, tt = pos_ref.shape

    @pl.when(j == 0)
    def _():
        xe_ref[...] = jnp.zeros_like(xe_ref)
        gs_ref[...] = jnp.zeros_like(gs_ref)

    rows = lax.broadcasted_iota(I32, (w, tt), 0)
    pos, aff, ht = pos_ref[...], aff_ref[...], h_ref[...]
    obase = bi * ne * (nt + 1) + j

    def put(e, base, hit, picked):
        cur = xe_ref[e, pl.ds(base, w), :].astype(F32)
        xe_ref[e, pl.ds(base, w), :] = (cur + picked).astype(BF16)
        gs_ref[e, pl.ds(base, w), :] += jnp.sum(jnp.where(hit, aff[e:e + 1], 0.0), axis=1, keepdims=True)

    wins = _tile_windows(off_sm, obase, nt + 1, pos, cap, w, rows)
    picked = jnp.dot(_onehots(wins), ht, preferred_element_type=F32)
    for e, (p1, base, hit) in enumerate(wins):
        put(e, base, hit, picked[e * w:(e + 1) * w])

    for e, (p1, base, _) in enumerate(wins):

        @pl.when(p1 > base + w)
        def _():
            def more(lo):
                base2, hit2 = _window_onehot(pos[e:e + 1], lo, cap, w, rows)
                oh = jnp.where(hit2, 1.0, 0.0).astype(BF16)
                put(e, base2, hit2, jnp.dot(oh, ht, preferred_element_type=F32))
                return base2 + w

            lax.while_loop(lambda lo: lo < p1, more, base + w)


def _gather(off_flat, bm, pos, aff, t0_blk, t, cap):
    b, _, d = bm.shape
    ne = pos.shape[1]
    tt = min(TOK_TILE, t)
    nt = t // tt
    w = min(SLOT_WINDOW, cap)
    kern = functools.partial(_gather_kernel, cap=cap, nt=nt, w=w)
    grid_spec = pltpu.PrefetchScalarGridSpec(
        num_scalar_prefetch=1,
        grid=(b, nt),
        in_specs=[pl.BlockSpec((None, tt, d), lambda i, j, o: (i, t0_blk * nt + j, 0)),
                  pl.BlockSpec((None, ne, tt), lambda i, j, o: (i, 0, j)),
                  pl.BlockSpec((None, ne, tt), lambda i, j, o: (i, 0, j))],
        out_specs=[pl.BlockSpec((ne, None, cap, d), lambda i, j, o: (0, i, 0, 0)),
                   pl.BlockSpec((ne, None, cap, 1), lambda i, j, o: (0, i, 0, 0))])
    return pl.pallas_call(
        kern,
        grid_spec=grid_spec,
        out_shape=[jax.ShapeDtypeStruct((ne, b, cap, d), BF16),
                   jax.ShapeDtypeStruct((ne, b, cap, 1), F32)],
        compiler_params=_cparams("parallel", "arbitrary"),
        name="moe_gather",
    )(off_flat, bm, pos, aff)


def _ffn_kernel(*refs, n_in, first_tile):
    xs, gss = refs[0:2 * n_in:2], refs[1:2 * n_in:2]
    w1_ref, w3_ref, w2_ref = refs[2 * n_in:2 * n_in + 3]
    outs = refs[2 * n_in + 3:4 * n_in + 3]
    w1b, w3b, w2b = refs[4 * n_in + 3:]
    i = pl.program_id(1)

    @pl.when(i == 0)
    def _():
        w1b[...] = w1_ref[...].astype(BF16)
        w3b[...] = w3_ref[...].astype(BF16)
        w2b[...] = w2_ref[...].astype(BF16)

    for k in range(n_in):

        @pl.when((i >= first_tile[k]) & (i < first_tile[k + 1]))
        def _():
            x = xs[k][...]
            h1 = jnp.dot(x, w1b[...], preferred_element_type=F32)
            h3 = jnp.dot(x, w3b[...], preferred_element_type=F32)
            hid = (h1 * _sigmoid(h1)) * h3
            ye = jnp.dot(hid.astype(BF16), w2b[...], preferred_element_type=F32) * gss[k][...]
            outs[2 * k][...], outs[2 * k + 1][...] = _split2(ye)


def _ffn(xes, gss, w1, w3, w2, layer):
    ne, _, d = xes[0].shape
    f = w1.shape[-1]
    trs = [min(x.shape[1], 512) for x in xes]
    first_tile = [0]
    for x, tr in zip(xes, trs):
        first_tile.append(first_tile[-1] + x.shape[1] // tr)

    def rows(k, width):
        lo, hi = first_tile[k], first_tile[k + 1]
        return pl.BlockSpec((None, trs[k], width), lambda e, i: (e, jnp.clip(i - lo, 0, hi - lo - 1), 0))

    in_specs, args = [], []
    for k, (x, g) in enumerate(zip(xes, gss)):
        in_specs += [rows(k, d), rows(k, 1)]
        args += [x, g]
    wspec = lambda shape: pl.BlockSpec((None, None) + shape, lambda e, i: (layer, e, 0, 0))
    outs = pl.pallas_call(
        functools.partial(_ffn_kernel, n_in=len(xes), first_tile=tuple(first_tile)),
        grid=(ne, first_tile[-1]),
        in_specs=in_specs + [wspec((d, f)), wspec((d, f)), wspec((f, d))],
        out_specs=[rows(k, d) for k in range(len(xes)) for _ in range(2)],
        out_shape=[jax.ShapeDtypeStruct(x.shape, BF16) for x in xes for _ in range(2)],
        scratch_shapes=[pltpu.VMEM((d, f), BF16), pltpu.VMEM((d, f), BF16), pltpu.VMEM((f, d), BF16)],
        compiler_params=_cparams("parallel", "arbitrary"),
        name="moe_ffn",
    )(*args, w1, w3, w2)
    return [outs[2 * k:2 * k + 2] for k in range(len(xes))]


def _combine_kernel(off_sm, x1_ref, mod_ref, pos_ref, yh_ref, yl_ref, o_ref, *, cap, nt, w):
    bi, j = pl.program_id(0), pl.program_id(2)
    ne, tt = pos_ref.shape
    rows = lax.broadcasted_iota(I32, (w, tt), 0)
    pos = pos_ref[...]
    wins = _tile_windows(off_sm, bi * ne * (nt + 1) + j, nt + 1, pos, cap, w, rows)
    onehot = _onehots(wins)
    tn = functools.partial(lax.dot_general, dimension_numbers=_TN, preferred_element_type=F32)
    slots = [ref[e, pl.ds(base, w), :] for ref in (yh_ref, yl_ref) for e, (_, base, _) in enumerate(wins)]
    o_ref[...] = tn(jnp.concatenate([onehot, onehot], axis=0), jnp.concatenate(slots, axis=0))

    for e, (p1, base, _) in enumerate(wins):

        @pl.when(p1 > base + w)
        def _():
            def more(lo):
                base2, hit2 = _window_onehot(pos[e:e + 1], lo, cap, w, rows)
                oh = jnp.where(hit2, 1.0, 0.0).astype(BF16)
                o_ref[...] += tn(oh, yh_ref[e, pl.ds(base2, w), :]) + tn(oh, yl_ref[e, pl.ds(base2, w), :])
                return base2 + w

            lax.while_loop(lambda lo: lo < p1, more, base + w)

    o_ref[...] = x1_ref[...] + mod_ref[...][5:6] * o_ref[...]


def _combine(off_flat, x1, mod, stream, pos, ye, t0_blk, t, cap):
    b, s, d = x1.shape
    ne = pos.shape[1]
    tt = min(TOK_TILE, t)
    nt = t // tt
    w = min(SLOT_WINDOW, cap)
    dh = d // COMBINE_SPLIT
    kern = functools.partial(_combine_kernel, cap=cap, nt=nt, w=w)
    grid_spec = pltpu.PrefetchScalarGridSpec(
        num_scalar_prefetch=1,
        grid=(b, COMBINE_SPLIT, nt),
        in_specs=[pl.BlockSpec((None, tt, dh), lambda i, c, j, o: (i, t0_blk * nt + j, c)),
                  pl.BlockSpec((None, None, 6, dh), lambda i, c, j, o: (i, stream, 0, c)),
                  pl.BlockSpec((None, ne, tt), lambda i, c, j, o: (i, 0, j)),
                  pl.BlockSpec((ne, None, cap, dh), lambda i, c, j, o: (0, i, 0, c)),
                  pl.BlockSpec((ne, None, cap, dh), lambda i, c, j, o: (0, i, 0, c))],
        out_specs=pl.BlockSpec((None, tt, dh), lambda i, c, j, o: (i, t0_blk * nt + j, c)))
    return pl.pallas_call(
        kern,
        grid_spec=grid_spec,
        out_shape=jax.ShapeDtypeStruct((b, s, d), F32),
        input_output_aliases={1: 0},
        compiler_params=_cparams("parallel", "parallel", "arbitrary"),
        name="moe_combine",
    )(off_flat, x1, mod, pos, *ye)


def _moe(x1, bm, logits_t, mod, streams, experts):
    b, _, d = x1.shape
    routed = []
    for _, t0_blk, t in streams:
        cap = CAPACITY_FACTOR * t // N_EXPERTS
        pos, aff, offs = _select(logits_t, t0_blk, t, cap)
        nt = t // min(TOK_TILE, t)
        off_flat = offs[:, :, :nt + 1].reshape(-1)
        xe, gs = _gather(off_flat, bm, pos, aff, t0_blk, t, cap)
        routed.append((cap, pos, off_flat, xe.reshape(N_EXPERTS, b * cap, d), gs.reshape(N_EXPERTS, b * cap, 1)))
    yes = _ffn([r[3] for r in routed], [r[4] for r in routed], *experts)
    for (stream, t0_blk, t), (cap, pos, off_flat, _, _), ye in zip(streams, routed, yes):
        ye = [y.reshape(N_EXPERTS, b, cap, d) for y in ye]
        x1 = _combine(off_flat, x1, mod, stream, pos, ye, t0_blk, t, cap)
    return x1


def _rwkv_stream_kernel(*refs, t_lat, has_vres):
    if has_vres:
        (x_ref, xp_ref, xn_ref, mod_ref, g1_ref, mu_ref, wr_ref, wk_ref, wv_ref, w1_ref, w2_ref, w0_ref,
         a1_ref, a2_ref, a0_ref, gg1_ref, gg2_ref, kk_ref, ka_ref, rk_ref, seg_ref, segt_ref,
         vf_ref, v0_ref, v1_ref, v2_ref,
         r_out, v_out, kk_out, lw_out, kd_out, b_out, g_out, bonus_out) = refs
    else:
        (x_ref, xp_ref, xn_ref, mod_ref, g1_ref, mu_ref, wr_ref, wk_ref, wv_ref, w1_ref, w2_ref, w0_ref,
         a1_ref, a2_ref, a0_ref, gg1_ref, gg2_ref, kk_ref, ka_ref, rk_ref, seg_ref, segt_ref,
         r_out, v_out, kk_out, lw_out, kd_out, b_out, g_out, bonus_out) = refs
    j = pl.program_id(1)
    mod = mod_ref[...]
    g1 = g1_ref[...]
    nm = lambda x: _norm_mod(x, g1, mod[1:2], mod[0:1])
    h = nm(x_ref[...])
    tm, d = h.shape
    hp = nm(xp_ref[...])[SUBLANE - 1:SUBLANE]
    hn = nm(xn_ref[...])[0:1]
    row = lax.broadcasted_iota(I32, (tm, 1), 0)
    grow = row + j * tm
    s_tot = pl.num_programs(1) * tm
    first = (grow == 0) | (grow == t_lat)
    last = (grow == t_lat - 1) | (grow == s_tot - 1)
    prev = jnp.where(row == 0, hp, pltpu.roll(h, 1, 0))
    prev = jnp.where(first, 0.0, prev)
    nxt = jnp.where(row == tm - 1, hn, pltpu.roll(h, tm - 1, 0))
    nxt = jnp.where(last, 0.0, nxt)
    xx = 0.5 * (prev + nxt) - h
    mu = mu_ref[...]
    xs = [h + xx * mu[m:m + 1] for m in range(6)]
    r = _bdot(xs[0], wr_ref[...])
    k = _bdot(xs[1], wk_ref[...])
    v = _bdot(xs[2], wv_ref[...])
    if has_vres:
        gate = _sigmoid(v0_ref[...] + _bdot(_bdot(xs[2], v1_ref[...]), v2_ref[...]))
        v = v + (vf_ref[...] - v) * gate
    wz = w0_ref[...] + _bdot(jnp.tanh(_bdot(xs[3], w1_ref[...])), w2_ref[...])
    lw = -math.exp(-0.5) * _sigmoid(wz)
    a = _sigmoid(a0_ref[...] + _bdot(_bdot(xs[4], a1_ref[...]), a2_ref[...]))
    g = _bdot(_sigmoid(_bdot(xs[5], gg1_ref[...])), gg2_ref[...])
    seg, segt = seg_ref[...], segt_ref[...]
    segsum = lambda t: _dot_exact_rhs01(_dot_exact_rhs01(t, seg), segt)
    kkr = k * kk_ref[...]
    kk = kkr / jnp.maximum(jnp.sqrt(segsum(kkr * kkr)), 1e-12)
    ka = ka_ref[...]
    kd0 = k * (1.0 + (a[:, :d] - 1.0) * ka)
    kd1 = k * (1.0 + (a[:, d:] - 1.0) * ka)
    r_out[...] = r
    v_out[...] = v
    kk_out[...] = kk
    lw_out[...] = lw
    kd_out[:, :d] = kd0
    kd_out[:, d:] = kd1
    b_out[:, :d] = kk * a[:, :d]
    b_out[:, d:] = kk * a[:, d:]
    g_out[...] = g
    bonus_out[...] = segsum(r * (kd0 + kd1) * rk_ref[...]) * v


def _rwkv_stream(x, mod, g1, wts, vres, t_lat):
    b, s, d = x.shape
    tm = TOK_TILE
    nlt = t_lat // tm
    nsub = tm // SUBLANE
    const = lambda a: pl.BlockSpec(a.shape, lambda i, j: (0,) * a.ndim)
    tok = lambda w: pl.BlockSpec((None, tm, w), lambda i, j: (i, j, 0))
    in_specs = [tok(d),
                pl.BlockSpec((None, SUBLANE, d), lambda i, j: (i, jnp.maximum(j * nsub - 1, 0), 0)),
                pl.BlockSpec((None, SUBLANE, d), lambda i, j: (i, jnp.minimum((j + 1) * nsub, s // SUBLANE - 1), 0)),
                pl.BlockSpec((None, None, 6, d), lambda i, j: (i, (j >= nlt).astype(I32), 0, 0)),
                const(g1)] + [const(a) for a in wts]
    args = [x, x, x, mod, g1] + list(wts)
    if vres is not None:
        vf, v0, v1, v2 = vres
        in_specs += [tok(d), const(v0), const(v1), const(v2)]
        args += [vf, v0, v1, v2]
    kern = functools.partial(_rwkv_stream_kernel, t_lat=t_lat, has_vres=vres is not None)
    widths = [d, d, d, 2 * d, 2 * d, 2 * d, d, d]
    return pl.pallas_call(
        kern,
        grid=(b, s // tm),
        in_specs=in_specs,
        out_specs=[tok(w) for w in widths],
        out_shape=[jax.ShapeDtypeStruct((b, s, w), F32) for w in widths],
        compiler_params=_cparams("parallel", "parallel"),
        name="rwkv_stream",
    )(*args)


def _scan_kernel(r_ref, v_ref, kk_ref, lw_ref, kd_ref, b_ref, y_ref, h_ref, *, reverse, nchunk):
    L = SCAN_CHUNK

    @pl.when(pl.program_id(2) == 0)
    def _():
        h_ref[...] = jnp.zeros_like(h_ref)

    n2 = 2 * L
    rr = lax.broadcasted_iota(I32, (n2, n2), 0)
    cc = lax.broadcasted_iota(I32, (n2, n2), 1)
    tr, tc = rr % L, cc % L
    same = (rr // L) == (cc // L)
    before = (tr < tc) if reverse else (tr > tc)
    strict = same & before
    incl = same & (before | (tr == tc))
    eye = rr == cc
    blk16 = (rr // 16) == (cc // 16)
    blk32 = (rr // 32) == (cc // 32)
    ri = lax.broadcasted_iota(I32, (L, L), 0)
    ci = lax.broadcasted_iota(I32, (L, L), 1)
    tri = jnp.where((ci >= ri) if reverse else (ci <= ri), 1.0, 0.0).astype(BF16)
    head0 = lax.broadcasted_iota(I32, (L, LANE), 1) < RW_HEAD

    def stack(x):
        return jnp.concatenate([jnp.where(head0, x, 0.0), jnp.where(head0, 0.0, x)], axis=0)

    order = list(range(nchunk - 1, -1, -1) if reverse else range(nchunk))
    npp = y_ref.shape[-1] // LANE
    units = [(pp, ch) for pp in range(npp) for ch in order]

    def each(f, *lists):
        return [f(*a) for a in zip(*lists)]

    def ld(ref):
        return [ref[pl.ds(ch * L, L), pl.ds(pp * LANE, LANE)] for pp, ch in units]

    dd = functools.partial(jnp.dot, preferred_element_type=F32)
    lw = ld(lw_ref)
    lws = each(_split3, lw)
    cs = each(lambda t: dd(tri, t[0]) + (dd(tri, t[1]) + dd(tri, t[2])), lws)
    total = each(lambda c: c[0:1] if reverse else c[L - 1:L], cs)
    gam = each(jnp.exp, cs)
    ginv = each(lambda c: jnp.exp(-c), cs)
    gprev = each(lambda c, l: jnp.exp(c - l), cs, lw)
    gend = each(lambda t, c: jnp.exp(t - c), total, cs)
    kk, bb, kd, r, v = ld(kk_ref), ld(b_ref), ld(kd_ref), ld(r_ref), ld(v_ref)
    mul_stack = lambda a, g: stack(a * g)
    rtm = each(mul_stack, r, gam)
    rts = each(_split2, rtm)
    kkm, btm, ktm = (each(_split2, each(mul_stack, a, g)) for a, g in ((kk, gprev), (bb, ginv), (kd, ginv)))
    bht, kht = (each(lambda a, g: _split2(stack(a * g).T), a, gend) for a in (bb, kd))
    vm = each(_split2, each(stack, v))
    n_ab = each(lambda a, c, e: _mm3_rows([a, c], e, _NT), kkm, rts, btm)
    m_ak = each(lambda a, c, e: _mm3_rows([a, c], e, _NT), kkm, rts, ktm)
    n = each(lambda t: jnp.where(strict, t[0], 0.0), n_ab)
    ab = each(lambda t: jnp.where(incl, t[1], 0.0), n_ab)
    m = each(lambda t: jnp.where(strict, t[0], 0.0), m_ak)
    ak = each(lambda t: jnp.where(incl, t[1], 0.0), m_ak)
    nd = each(lambda t: jnp.where(blk16, t, 0.0), n)
    x = each(lambda t: jnp.where(eye, 1.0, 0.0) - t, nd)
    pw = nd
    for _ in range(3):
        pw = each(lambda t: _bdot(t, t), pw)
        x = each(lambda a, c: a + _bdot(a, c), x, pw)
    ident = jnp.where(eye, 1.0, 0.0)
    for inner, outer in ((blk16, blk32), (blk32, same)):
        c = each(lambda t: jnp.where(outer & jnp.logical_not(inner), t, 0.0), n)
        e = each(lambda a: a - ident, x)
        ec = each(_bdot, e, c)
        x = each(lambda a, t, u, s: a - t - (u + _bdot(t + u, s)), x, c, ec, e)
    xs = each(_split2, x)
    abs_ = each(_split2, ab)
    mv_akv_khv = each(lambda a, c, e, f: _mm3_rows([_split2(a), _split2(c), e], f), m, ak, kht, vm)
    cat = lambda a, c: tuple(jnp.concatenate([s, t], axis=1) for s, t in zip(a, c))
    wu = each(lambda a, c, t: _split2(_mm3(a, cat(c, _split2(t[0])))), xs, kkm, mv_akv_khv)
    bw_abw = each(lambda a, c, e: _mm3_rows([a, c], e), bht, abs_, wu)
    p = each(lambda t, a: jnp.where(eye, jnp.exp(t), 0.0) - a[0][:, :LANE], total, bw_abw)
    q = each(lambda t, a: t[2] - a[0][:, LANE:], mv_akv_khv, bw_abw)
    rres = each(lambda t, a: t - a[1][:, :LANE], rtm, bw_abw)
    y0 = each(lambda t, a: t[1] - a[1][:, LANE:], mv_akv_khv, bw_abw)
    for i, (pp, ch) in enumerate(units):
        y, hn = _mm3_rows([_split2(rres[i]), _split2(p[i])], _split2(h_ref[pp]))
        y = y + y0[i]
        y_ref[pl.ds(ch * L, L), pl.ds(pp * LANE, LANE)] = y[:L] + y[L:]
        h_ref[pp] = hn + q[i]


def _scan(r, v, kk, lw, kd, bb, t_lat, reverse):
    b, s, d = r.shape
    tb = TOK_TILE
    nl, nc = t_lat // tb, (s - t_lat) // tb
    z = 1 if reverse else 0

    def blk(c):
        if reverse:
            return jnp.where(c < nc, nl + nc - 1 - c, nl - 1 - (c - nc))
        return jnp.where(c < nc, nl + c, c - nc)

    wl = SCAN_PAIRS * LANE
    ngrp = d // wl
    shared = pl.BlockSpec((None, tb, wl), lambda i, p, c: (i, blk(c), p))
    dirn = pl.BlockSpec((None, tb, wl), lambda i, p, c: (i, blk(c), z * ngrp + p))
    kern = functools.partial(_scan_kernel, reverse=reverse, nchunk=tb // SCAN_CHUNK)
    return pl.pallas_call(
        kern,
        grid=(b, ngrp, nl + nc),
        in_specs=[shared, shared, shared, dirn, dirn, dirn],
        out_specs=pl.BlockSpec((None, tb, wl), lambda i, p, c: (i, blk(c), p)),
        out_shape=jax.ShapeDtypeStruct((b, s, d), F32),
        scratch_shapes=[pltpu.VMEM((SCAN_PAIRS, LANE, LANE), F32)],
        compiler_params=_cparams("parallel", "parallel", "arbitrary"),
        name="wkv_scan_bwd" if reverse else "wkv_scan_fwd",
    )(r, v, kk, lw, kd, bb)


def _final_kernel(x_ref, g_ref, o_ref):
    o_ref[...] = _rms(x_ref[...], g_ref[...])


def _final_norm(x, g, t_lat):
    b, _, d = x.shape
    tm = TOK_TILE
    return pl.pallas_call(
        _final_kernel,
        grid=(b, t_lat // tm),
        in_specs=[pl.BlockSpec((None, tm, d), lambda i, j: (i, j, 0)),
                  pl.BlockSpec(g.shape, lambda i, j: (0, 0))],
        out_specs=pl.BlockSpec((None, tm, d), lambda i, j: (i, j, 0)),
        out_shape=jax.ShapeDtypeStruct((b, t_lat, d), F32),
        compiler_params=_cparams("parallel", "parallel"),
        name="final_norm",
    )(x, g)


def _rope_tables(t_lat, t_ctx):
    rows = t_lat // GRID_W
    r = jnp.broadcast_to(jnp.arange(rows)[:, None], (rows, GRID_W)).reshape(-1).astype(F32)
    col = jnp.broadcast_to(jnp.arange(GRID_W)[None, :], (rows, GRID_W)).reshape(-1).astype(F32)
    n_freq = MLA_ROPE // 4
    inv = ROPE_THETA ** (-jnp.arange(n_freq, dtype=F32) / n_freq)
    ang = jnp.concatenate([r[:, None] * inv, col[:, None] * inv], axis=-1)
    cos, sin = jnp.cos(ang), jnp.sin(ang)
    cosf = jnp.concatenate([cos, cos, cos, cos], axis=-1)
    sins = jnp.concatenate([-sin, sin, -sin, sin], axis=-1)
    cosf = jnp.concatenate([cosf, jnp.ones((t_ctx, LANE), F32)], axis=0)
    sins = jnp.concatenate([sins, jnp.zeros((t_ctx, LANE), F32)], axis=0)
    return cosf, sins


def _blockdiag2(w):
    z = jnp.zeros_like(w[0])
    return jnp.concatenate([jnp.concatenate([w[0], z], axis=1), jnp.concatenate([z, w[1]], axis=1)], axis=0)


def kernel(x, c, ctx, c_ctx, mod_w, mod_b, norm1_g, norm2_g, final_g, mla_w_in, mla_g_q, mla_g_kv, mla_w_uq, mla_w_ukv, mla_w_o, rw_mu, rw_w_rkv, rw_w0, rw_w1, rw_w2, rw_a0, rw_a1, rw_a2, rw_v0, rw_v1, rw_v2, rw_g1, rw_g2, rw_k_k, rw_k_a, rw_r_k, rw_lnx_g, rw_lnx_b, rw_w_o, moe_router, moe_w1, moe_w3, moe_w2):
    b, t_lat, d = x.shape
    t_ctx = ctx.shape[1]
    depth = mod_w.shape[0]
    assert b + 1 <= SUBLANE and t_lat % TOK_TILE == 0 and t_ctx % TOK_TILE == 0 and d % LANE == 0
    n_lat_tiles = t_lat // TOK_TILE
    ctx_blk = t_lat // t_ctx
    assert ctx_blk * t_ctx == t_lat

    cc = jnp.concatenate([c, c_ctx[None], jnp.zeros((SUBLANE - b - 1, d), F32)], axis=0)
    mods = _modulation(cc, mod_w, mod_b).reshape(depth, SUBLANE, 6, d)
    mod_all = jnp.stack([mods[:, :b], jnp.broadcast_to(mods[:, b:b + 1], (depth, b, 6, d))], axis=2)

    xs = jnp.concatenate([x, ctx], axis=1)
    cosf, sins = _rope_tables(t_lat, t_ctx)
    row = lambda a: a.reshape(1, -1)
    seg = (jnp.arange(d)[:, None] // RW_HEAD == jnp.arange(LANE)[None, :]).astype(BF16)
    segt = seg.T

    v_first = None
    for layer in range(depth):
        need_ctx = layer < depth - 1
        mod = mod_all[layer]
        j = layer // 2
        if layer % 2 == 0:
            win = jnp.pad(mla_w_in[j], ((0, 0), (0, 4 * LANE - mla_w_in.shape[-1]))).astype(BF16)
            wuq = mla_w_uq[j].reshape(MLA_Q_LORA, MLA_HEADS, MLA_NOPE + MLA_ROPE)
            wuq = jnp.concatenate([wuq[..., :MLA_NOPE].reshape(MLA_Q_LORA, -1),
                                   wuq[..., MLA_NOPE:].reshape(MLA_Q_LORA, -1)], axis=1).astype(BF16)
            q, k, v = _mla_proj(xs, mod, row(norm1_g[layer]), win, row(mla_g_q[j]), row(mla_g_kv[j]),
                                wuq, mla_w_ukv[j].astype(BF16), cosf, sins, n_lat_tiles)
            mix = [_attention(q, k, v, t_lat)]
            wo = mla_w_o[j].astype(BF16)
        else:
            lora = lambda w: jnp.concatenate([w[0], w[1]], axis=1).astype(BF16)
            wts = [rw_mu[j], rw_w_rkv[j, 0].astype(BF16), rw_w_rkv[j, 1].astype(BF16), rw_w_rkv[j, 2].astype(BF16),
                   lora(rw_w1[j]), _blockdiag2(rw_w2[j]).astype(BF16), rw_w0[j].reshape(1, -1),
                   lora(rw_a1[j]), _blockdiag2(rw_a2[j]).astype(BF16), rw_a0[j].reshape(1, -1),
                   rw_g1[j].astype(BF16), rw_g2[j].astype(BF16),
                   row(rw_k_k[j]), row(rw_k_a[j]), row(rw_r_k[j]), seg, segt]
            vres = None
            if j > 0:
                pad = LANE - rw_v1.shape[-1]
                vres = (v_first, row(rw_v0[j - 1]), jnp.pad(rw_v1[j - 1], ((0, 0), (0, pad))).astype(BF16),
                        jnp.pad(rw_v2[j - 1], ((0, pad), (0, 0))).astype(BF16))
            r, v, kk, lw, kd, bb, g, bonus = _rwkv_stream(xs, mod, row(norm1_g[layer]), wts, vres, t_lat)
            if j == 0:
                v_first = v
            yf = _scan(r, v, kk, lw, kd, bb, t_lat, reverse=False)
            yb = _scan(r, v, kk, lw, kd, bb, t_lat, reverse=True)
            mix = [yf, yb, bonus, g, row(rw_lnx_g[j]), row(rw_lnx_b[j]), seg, segt]
            wo = rw_w_o[j].astype(BF16)
        x1, bm, logits_t = _post_mixer(xs, mix, wo, mod, row(norm2_g[layer]), moe_router[layer].T.astype(BF16),
                                       n_lat_tiles)
        experts = (moe_w1, moe_w3, moe_w2, layer)
        streams = [(0, 0, t_lat)] + ([(1, ctx_blk, t_ctx)] if need_ctx else [])
        xs = _moe(x1, bm, logits_t, mod, streams, experts)
    return _final_norm(xs, row(final_g), t_lat)
```

```python
import functools
import math

import jax
import jax.numpy as jnp
from jax import lax
from jax.experimental import pallas as pl
from jax.experimental.pallas import tpu as pltpu

F32, BF16, I32 = jnp.float32, jnp.bfloat16, jnp.int32

GRID_W = 64
MLA_HEADS, MLA_NOPE, MLA_ROPE, MLA_V = 8, 128, 64, 128
MLA_Q_LORA, MLA_KV_LORA = 256, 128
MLA_SCALE = (MLA_NOPE + MLA_ROPE) ** -0.5
ROPE_THETA = 10000.0
RW_HEAD = 64
RW_LNX_EPS = 64e-5
N_EXPERTS = 16
CAPACITY_FACTOR = 2
RMS_EPS = 1e-6

LANE = 128
SUBLANE = 8
TOK_TILE = 256
WIN_ALIGN = 16
SLOT_WINDOW = 64
SELECT_BISECTIONS = 26
COMBINE_SPLIT = 2
ATTN_TILE = 1024
ATTN_ROWS = 256
SCAN_CHUNK = 64
SCAN_PAIRS = 4
VMEM_LIMIT = 56 * 1024 * 1024


def _cparams(*sem):
    return pltpu.CompilerParams(dimension_semantics=sem, vmem_limit_bytes=VMEM_LIMIT)


def _bdot(a, b):
    return jnp.dot(a.astype(BF16), b.astype(BF16), preferred_element_type=F32)


_NT = (((1,), (1,)), ((), ()))
_TN = (((0,), (0,)), ((), ()))
_NN = (((1,), (0,)), ((), ()))


def _split2(x):
    hi = x.astype(BF16)
    lo = (x - hi.astype(F32)).astype(BF16)
    return hi, lo


def _split3(x):
    hi = x.astype(BF16)
    r = x - hi.astype(F32)
    mid = r.astype(BF16)
    lo = (r - mid.astype(F32)).astype(BF16)
    return hi, mid, lo


def _mm3(a, b, dims=_NN):
    (lc,), (rc,) = dims[0]
    lhs = jnp.concatenate([a[0], a[1], a[0]], axis=lc)
    rhs = jnp.concatenate([b[0], b[0], b[1]], axis=rc)
    return lax.dot_general(lhs, rhs, dims, preferred_element_type=F32)


def _mm3_rows(lhs, b, dims=_NN):
    his = jnp.concatenate([a[0] for a in lhs], axis=0)
    los = jnp.concatenate([a[1] for a in lhs], axis=0)
    res = _mm3((his, los), b, dims)
    outs, o = [], 0
    for a in lhs:
        outs.append(res[o:o + a[0].shape[0]])
        o += a[0].shape[0]
    return outs


def _dot_exact_rhs01(x, m01):
    h, l = _split2(x)
    return jnp.dot(jnp.concatenate([h, l], axis=1), jnp.concatenate([m01, m01], axis=0),
                   preferred_element_type=F32)


def _rms(x, g):
    return (x * lax.rsqrt(jnp.mean(x * x, axis=-1, keepdims=True) + RMS_EPS)) * g


def _norm_mod(x, g, scale, shift):
    return _rms(x, g) * (1.0 + scale) + shift


def _sigmoid(x):
    return 1.0 / (1.0 + jnp.exp(-x))


def _mod_kernel(c_ref, w_ref, b_ref, o_ref):
    s = c_ref[...]
    s = s * _sigmoid(s)
    o_ref[...] = _bdot(s, w_ref[...]) + b_ref[...]


def _modulation(cc, mod_w, mod_b):
    depth, d, n = mod_w.shape
    tn = n // 4
    return pl.pallas_call(
        _mod_kernel,
        grid=(depth, n // tn),
        in_specs=[pl.BlockSpec((SUBLANE, d), lambda l, j: (0, 0)),
                  pl.BlockSpec((None, d, tn), lambda l, j: (l, 0, j)),
                  pl.BlockSpec((None, 1, tn), lambda l, j: (l, 0, j))],
        out_specs=pl.BlockSpec((None, SUBLANE, tn), lambda l, j: (l, 0, j)),
        out_shape=jax.ShapeDtypeStruct((depth, SUBLANE, n), F32),
        compiler_params=_cparams("arbitrary", "arbitrary"),
        name="modulation",
    )(cc, mod_w, mod_b.reshape(depth, 1, n))


def _rope128(x, cosf, sins):
    lane = lax.broadcasted_iota(I32, x.shape, 1)
    first = (lane % MLA_ROPE) < (MLA_ROPE // 2)
    rot = jnp.where(first, pltpu.roll(x, LANE - MLA_ROPE // 2, 1), pltpu.roll(x, MLA_ROPE // 2, 1))
    return x * cosf + rot * sins


def _mla_proj_kernel(x_ref, mod_ref, g1_ref, win_ref, gq_ref, gkv_ref, wuq_ref, wukv_ref,
                     cos_ref, sin_ref, q_ref, k_ref, v_ref):
    mod = mod_ref[...]
    h = _norm_mod(x_ref[...], g1_ref[...], mod[1:2], mod[0:1])
    z = _bdot(h, win_ref[...])
    cq = _rms(z[:, :MLA_Q_LORA], gq_ref[...])
    ckv = _rms(z[:, MLA_Q_LORA:MLA_Q_LORA + MLA_KV_LORA], gkv_ref[...])
    kr = z[:, MLA_Q_LORA + MLA_KV_LORA:]
    q = _bdot(cq, wuq_ref[...])
    kv = _bdot(ckv, wukv_ref[...])
    cosf, sins = cos_ref[...], sin_ref[...]
    kr_lo = _rope128(kr, cosf, sins)
    kr_hi = pltpu.roll(kr_lo, MLA_ROPE, 1)
    nope_w = MLA_HEADS * MLA_NOPE
    for g in range(MLA_HEADS // 2):
        qr = _rope128(q[:, nope_w + LANE * g:nope_w + LANE * (g + 1)], cosf, sins)
        for hh in (2 * g, 2 * g + 1):
            qn = q[:, MLA_NOPE * hh:MLA_NOPE * (hh + 1)]
            q_ref[hh] = jnp.concatenate([qn, qr], axis=1).astype(BF16)
    for hh in range(MLA_HEADS):
        base = (MLA_NOPE + MLA_V) * hh
        krh = kr_lo if hh % 2 == 0 else kr_hi
        k_ref[hh] = jnp.concatenate([kv[:, base:base + MLA_NOPE], krh], axis=1).astype(BF16)
        v_ref[hh] = kv[:, base + MLA_NOPE:base + MLA_NOPE + MLA_V].astype(BF16)


def _mla_proj(x, mod, g1, win, gq, gkv, wuq, wukv, cosf, sins, n_lat_tiles):
    b, s, d = x.shape
    tm = TOK_TILE
    const = lambda shape: pl.BlockSpec(shape, lambda i, j: (0,) * len(shape))
    hspec = lambda w: pl.BlockSpec((None, MLA_HEADS, tm, w), lambda i, j: (i, 0, j, 0))
    return pl.pallas_call(
        _mla_proj_kernel,
        grid=(b, s // tm),
        in_specs=[pl.BlockSpec((None, tm, d), lambda i, j: (i, j, 0)),
                  pl.BlockSpec((None, None, 6, d), lambda i, j: (i, (j >= n_lat_tiles).astype(I32), 0, 0)),
                  const(g1.shape), const(win.shape), const(gq.shape), const(gkv.shape),
                  const(wuq.shape), const(wukv.shape),
                  pl.BlockSpec((tm, LANE), lambda i, j: (j, 0)),
                  pl.BlockSpec((tm, LANE), lambda i, j: (j, 0))],
        out_specs=[hspec(2 * LANE), hspec(2 * LANE), hspec(MLA_V)],
        out_shape=[jax.ShapeDtypeStruct((b, MLA_HEADS, s, 2 * LANE), BF16),
                   jax.ShapeDtypeStruct((b, MLA_HEADS, s, 2 * LANE), BF16),
                   jax.ShapeDtypeStruct((b, MLA_HEADS, s, MLA_V), BF16)],
        compiler_params=_cparams("parallel", "parallel"),
        name="mla_proj",
    )(x, mod, g1, win, gq, gkv, wuq, wukv, cosf, sins)


def _attn_kernel(q_ref, k_ref, v_ref, o_ref, *, n_lat_tiles, t_lat):
    j = pl.program_id(2)
    t_ctx = k_ref.shape[0] - t_lat

    def softmax_pv(s, v):
        m = jnp.max(s, axis=-1, keepdims=True)
        p = jnp.exp2((s - m) * (MLA_SCALE * math.log2(math.e)))
        l = jnp.sum(p, axis=-1, keepdims=True)
        o = jnp.dot(p.astype(BF16), v, preferred_element_type=F32)
        return (o / l).astype(o_ref.dtype)

    @pl.when(j < n_lat_tiles)
    def _():
        for r0 in range(0, o_ref.shape[0], ATTN_ROWS):
            s = lax.dot_general(q_ref[r0:r0 + ATTN_ROWS, :], k_ref[...], _NT, preferred_element_type=F32)
            o_ref[r0:r0 + ATTN_ROWS, :] = softmax_pv(s, v_ref[...])

    @pl.when(j >= n_lat_tiles)
    def _():
        s = lax.dot_general(q_ref[:t_ctx, :], k_ref[t_lat:, :], _NT, preferred_element_type=F32)
        o_ref[:t_ctx, :] = softmax_pv(s, v_ref[t_lat:, :])


def _attention(q, k, v, t_lat):
    b, nh, s, _ = q.shape
    tq = min(ATTN_TILE, t_lat)
    assert t_lat % tq == 0 and s - t_lat <= tq
    kern = functools.partial(_attn_kernel, n_lat_tiles=t_lat // tq, t_lat=t_lat)
    return pl.pallas_call(
        kern,
        grid=(b, nh, pl.cdiv(s, tq)),
        in_specs=[pl.BlockSpec((None, None, tq, q.shape[-1]), lambda i, h, j: (i, h, j, 0)),
                  pl.BlockSpec((None, None, s, k.shape[-1]), lambda i, h, j: (i, h, 0, 0)),
                  pl.BlockSpec((None, None, s, v.shape[-1]), lambda i, h, j: (i, h, 0, 0))],
        out_specs=pl.BlockSpec((None, tq, MLA_V), lambda i, h, j: (i, j, h)),
        out_shape=jax.ShapeDtypeStruct((b, s, nh * MLA_V), BF16),
        compiler_params=_cparams("parallel", "parallel", "arbitrary"),
        name="mla_attention",
    )(q, k, v)


def _post_mixer_kernel(*refs, n_mix):
    x_ref, mix = refs[0], refs[1:1 + n_mix]
    wo_ref, mod_ref, g2_ref, rt_ref, x1_ref, bm_ref, lg_ref = refs[1 + n_mix:]
    if n_mix == 1:
        y = mix[0][...]
    else:
        yf_ref, yb_ref, bonus_ref, g_ref, lng_ref, lnb_ref, seg_ref, segt_ref = mix
        seg, segt = seg_ref[...], segt_ref[...]
        segmean = lambda t: _dot_exact_rhs01(_dot_exact_rhs01(t, seg), segt) * (1.0 / RW_HEAD)
        yy = yf_ref[...] + yb_ref[...]
        dlt = yy - segmean(yy)
        var = segmean(dlt * dlt)
        yn = (dlt * lax.rsqrt(var + RW_LNX_EPS)) * lng_ref[...] + lnb_ref[...]
        y = ((yn + bonus_ref[...]) * g_ref[...]).astype(BF16)
    mod = mod_ref[...]
    x1 = x_ref[...] + mod[2:3] * jnp.dot(y, wo_ref[...], preferred_element_type=F32)
    x1_ref[...] = x1
    bm = _norm_mod(x1, g2_ref[...], mod[4:5], mod[3:4]).astype(BF16)
    bm_ref[...] = bm
    lg_ref[...] = lax.dot_general(rt_ref[...], bm, _NT, preferred_element_type=F32)


def _post_mixer(x, mix, wo, mod, g2, router_t, n_lat_tiles):
    b, s, d = x.shape
    tm = TOK_TILE
    const = lambda shape: pl.BlockSpec(shape, lambda i, j: (0,) * len(shape))
    mix_specs = [pl.BlockSpec((None, tm, a.shape[-1]), lambda i, j: (i, j, 0)) if a.ndim == 3 else const(a.shape)
                 for a in mix]
    return pl.pallas_call(
        functools.partial(_post_mixer_kernel, n_mix=len(mix)),
        grid=(b, s // tm),
        in_specs=[pl.BlockSpec((None, tm, d), lambda i, j: (i, j, 0))] + mix_specs + [
                  const(wo.shape),
                  pl.BlockSpec((None, None, 6, d), lambda i, j: (i, (j >= n_lat_tiles).astype(I32), 0, 0)),
                  const(g2.shape), const(router_t.shape)],
        out_specs=[pl.BlockSpec((None, tm, d), lambda i, j: (i, j, 0)),
                   pl.BlockSpec((None, tm, d), lambda i, j: (i, j, 0)),
                   pl.BlockSpec((None, N_EXPERTS, tm), lambda i, j: (i, 0, j))],
        out_shape=[jax.ShapeDtypeStruct((b, s, d), F32),
                   jax.ShapeDtypeStruct((b, s, d), BF16),
                   jax.ShapeDtypeStruct((b, N_EXPERTS, s), F32)],
        compiler_params=_cparams("parallel", "parallel"),
        name="post_mixer",
    )(x, *mix, wo, mod, g2, router_t)


def _select_kernel(lg_ref, pos_ref, aff_ref, off_ref, *, cap, tt):
    lg = lg_ref[...]
    ne, t = lg.shape
    m = jnp.max(lg, axis=0, keepdims=True)
    ex = jnp.exp(lg - m)
    aff = ex / jnp.sum(ex, axis=0, keepdims=True)
    aff_ref[...] = aff

    def enough(v):
        return jnp.sum(jnp.where(aff >= v, 1.0, 0.0), axis=1, keepdims=True) >= cap

    hi = jnp.full((ne, 1), 2.0, F32)
    for shift in (64, 32, 16, 8, 4, 2, 1):
        cand = hi * (2.0 ** -shift)
        hi = jnp.where(enough(cand), hi, cand)
    half = 0.5 * hi
    lo = jnp.where(enough(half), half, 0.0)

    def bisect(_, lohi):
        lo, hi = lohi
        mid = 0.5 * (lo + hi)
        ok = enough(mid)
        return jnp.where(ok, mid, lo), jnp.where(ok, hi, mid)

    lo, hi = lax.fori_loop(0, SELECT_BISECTIONS, bisect, (lo, hi))
    gt = aff >= hi
    eq = (aff >= lo) & jnp.logical_not(gt)
    need = cap - jnp.sum(jnp.where(gt, 1.0, 0.0), axis=1, keepdims=True)

    nch = t // LANE
    tri = (lax.broadcasted_iota(I32, (LANE, LANE), 0) <= lax.broadcasted_iota(I32, (LANE, LANE), 1))
    tri = jnp.where(tri, 1.0, 0.0).astype(BF16)

    def chunk(a, c):
        return a[:, c * LANE:(c + 1) * LANE]

    off = jnp.zeros((ne, 1), F32)
    sel = []
    for c in range(nch):
        eqc = jnp.where(chunk(eq, c), 1.0, 0.0)
        inc = jnp.dot(eqc.astype(BF16), tri, preferred_element_type=F32)
        rank = inc - eqc + off
        off = off + inc[:, LANE - 1:LANE]
        sel.append(jnp.where(chunk(gt, c) | (chunk(eq, c) & (rank < need)), 1.0, 0.0))

    lane = lax.broadcasted_iota(I32, (ne, LANE), 1)
    offs = jnp.zeros((ne, LANE), F32)
    off = jnp.zeros((ne, 1), F32)
    per_tile = tt // LANE
    for c in range(nch):
        if c % per_tile == 0:
            offs = jnp.where(lane == c // per_tile, off, offs)
        inc = jnp.dot(sel[c].astype(BF16), tri, preferred_element_type=F32)
        pos = inc - sel[c] + off
        off = off + inc[:, LANE - 1:LANE]
        pos_ref[:, c * LANE:(c + 1) * LANE] = jnp.where(sel[c] > 0.0, pos, -1.0).astype(I32)
    offs = jnp.where(lane == nch // per_tile, off, offs)
    off_ref[...] = offs.astype(I32)


def _select(logits_t, t0_blk, t, cap):
    b, ne, _ = logits_t.shape
    tt = min(TOK_TILE, t)
    kern = functools.partial(_select_kernel, cap=cap, tt=tt)
    return pl.pallas_call(
        kern,
        grid=(b,),
        in_specs=[pl.BlockSpec((None, ne, t), lambda i: (i, 0, t0_blk))],
        out_specs=[pl.BlockSpec((None, ne, t), lambda i: (i, 0, 0)),
                   pl.BlockSpec((None, ne, t), lambda i: (i, 0, 0)),
                   pl.BlockSpec((None, ne, LANE), lambda i: (i, 0, 0))],
        out_shape=[jax.ShapeDtypeStruct((b, ne, t), I32),
                   jax.ShapeDtypeStruct((b, ne, t), F32),
                   jax.ShapeDtypeStruct((b, ne, LANE), I32)],
        compiler_params=_cparams("parallel"),
        name="moe_select",
    )(logits_t)


def _window_onehot(posr, lo, cap, w, rows):
    base = pl.multiple_of(jnp.minimum((lo // WIN_ALIGN) * WIN_ALIGN, cap - w), WIN_ALIGN)
    hit = ((posr - base) == rows) & (posr >= lo)
    return base, hit


def _tile_windows(off_sm, obase, stride, pos, cap, w, rows):
    out = []
    for e in range(pos.shape[0]):
        p0, p1 = off_sm[obase + e * stride], off_sm[obase + e * stride + 1]
        base, hit = _window_onehot(pos[e:e + 1], p0, cap, w, rows)
        out.append((p1, base, hit))
    return out


def _onehots(wins):
    return jnp.concatenate([jnp.where(hit, 1.0, 0.0).astype(BF16) for _, _, hit in wins], axis=0)


def _gather_kernel(off_sm, h_ref, pos_ref, aff_ref, xe_ref, gs_ref, *, cap, nt, w):
    bi, j = pl.program_id(0), pl.program_id(1)
    ne, tt = pos_ref.shape

    @pl.when(j == 0)
    def _():
        xe_ref[...] = jnp.zeros_like(xe_ref)
        gs_ref[...] = jnp.zeros_like(gs_ref)

    rows = lax.broadcasted_iota(I32, (w, tt), 0)
    pos, aff, ht = pos_ref[...], aff_ref[...], h_ref[...]
    obase = bi * ne * (nt + 1) + j

    def put(e, base, hit, picked):
        cur = xe_ref[e, pl.ds(base, w), :].astype(F32)
        xe_ref[e, pl.ds(base, w), :] = (cur + picked).astype(BF16)
        gs_ref[e, pl.ds(base, w), :] += jnp.sum(jnp.where(hit, aff[e:e + 1], 0.0), axis=1, keepdims=True)

    wins = _tile_windows(off_sm, obase, nt + 1, pos, cap, w, rows)
    picked = jnp.dot(_onehots(wins), ht, preferred_element_type=F32)
    for e, (p1, base, hit) in enumerate(wins):
        put(e, base, hit, picked[e * w:(e + 1) * w])

    for e, (p1, base, _) in enumerate(wins):

        @pl.when(p1 > base + w)
        def _():
            def more(lo):
                base2, hit2 = _window_onehot(pos[e:e + 1], lo, cap, w, rows)
                oh = jnp.where(hit2, 1.0, 0.0).astype(BF16)
                put(e, base2, hit2, jnp.dot(oh, ht, preferred_element_type=F32))
                return base2 + w

            lax.while_loop(lambda lo: lo < p1, more, base + w)


def _gather(off_flat, bm, pos, aff, t0_blk, t, cap):
    b, _, d = bm.shape
    ne = pos.shape[1]
    tt = min(TOK_TILE, t)
    nt = t // tt
    w = min(SLOT_WINDOW, cap)
    kern = functools.partial(_gather_kernel, cap=cap, nt=nt, w=w)
    grid_spec = pltpu.PrefetchScalarGridSpec(
        num_scalar_prefetch=1,
        grid=(b, nt),
        in_specs=[pl.BlockSpec((None, tt, d), lambda i, j, o: (i, t0_blk * nt + j, 0)),
                  pl.BlockSpec((None, ne, tt), lambda i, j, o: (i, 0, j)),
                  pl.BlockSpec((None, ne, tt), lambda i, j, o: (i, 0, j))],
        out_specs=[pl.BlockSpec((ne, None, cap, d), lambda i, j, o: (0, i, 0, 0)),
                   pl.BlockSpec((ne, None, cap, 1), lambda i, j, o: (0, i, 0, 0))])
    return pl.pallas_call(
        kern,
        grid_spec=grid_spec,
        out_shape=[jax.ShapeDtypeStruct((ne, b, cap, d), BF16),
                   jax.ShapeDtypeStruct((ne, b, cap, 1), F32)],
        compiler_params=_cparams("parallel", "arbitrary"),
        name="moe_gather",
    )(off_flat, bm, pos, aff)


def _ffn_kernel(*refs, n_in, first_tile):
    xs, gss = refs[0:2 * n_in:2], refs[1:2 * n_in:2]
    w1_ref, w3_ref, w2_ref = refs[2 * n_in:2 * n_in + 3]
    outs = refs[2 * n_in + 3:4 * n_in + 3]
    w1b, w3b, w2b = refs[4 * n_in + 3:]
    i = pl.program_id(1)

    def tile(k, w1v, w3v, w2v):
        x = xs[k][...]
        h1 = jnp.dot(x, w1v, preferred_element_type=F32)
        h3 = jnp.dot(x, w3v, preferred_element_type=F32)
        hid = (h1 * _sigmoid(h1)) * h3
        ye = jnp.dot(hid.astype(BF16), w2v, preferred_element_type=F32) * gss[k][...]
        outs[2 * k][...], outs[2 * k + 1][...] = _split2(ye)

    @pl.when(i == 0)
    def _():
        w1v, w3v, w2v = (w[...].astype(BF16) for w in (w1_ref, w3_ref, w2_ref))
        w1b[...], w3b[...], w2b[...] = w1v, w3v, w2v
        tile(0, w1v, w3v, w2v)

    for k in range(n_in):

        @pl.when((i >= max(first_tile[k], 1)) & (i < first_tile[k + 1]))
        def _():
            tile(k, w1b[...], w3b[...], w2b[...])


def _ffn(xes, gss, w1, w3, w2, layer):
    ne, _, d = xes[0].shape
    f = w1.shape[-1]
    trs = [min(x.shape[1], 512) for x in xes]
    first_tile = [0]
    for x, tr in zip(xes, trs):
        first_tile.append(first_tile[-1] + x.shape[1] // tr)

    def rows(k, width):
        lo, hi = first_tile[k], first_tile[k + 1]
        return pl.BlockSpec((None, trs[k], width), lambda e, i: (e, jnp.clip(i - lo, 0, hi - lo - 1), 0))

    in_specs, args = [], []
    for k, (x, g) in enumerate(zip(xes, gss)):
        in_specs += [rows(k, d), rows(k, 1)]
        args += [x, g]
    wspec = lambda shape: pl.BlockSpec((None, None) + shape, lambda e, i: (layer, e, 0, 0))
    outs = pl.pallas_call(
        functools.partial(_ffn_kernel, n_in=len(xes), first_tile=tuple(first_tile)),
        grid=(ne, first_tile[-1]),
        in_specs=in_specs + [wspec((d, f)), wspec((d, f)), wspec((f, d))],
        out_specs=[rows(k, d) for k in range(len(xes)) for _ in range(2)],
        out_shape=[jax.ShapeDtypeStruct(x.shape, BF16) for x in xes for _ in range(2)],
        scratch_shapes=[pltpu.VMEM((d, f), BF16), pltpu.VMEM((d, f), BF16), pltpu.VMEM((f, d), BF16)],
        compiler_params=_cparams("parallel", "arbitrary"),
        name="moe_ffn",
    )(*args, w1, w3, w2)
    return [outs[2 * k:2 * k + 2] for k in range(len(xes))]


def _combine_kernel(off_sm, x1_ref, mod_ref, pos_ref, yh_ref, yl_ref, o_ref, *, cap, nt, w):
    bi, j = pl.program_id(0), pl.program_id(2)
    ne, tt = pos_ref.shape
    rows = lax.broadcasted_iota(I32, (w, tt), 0)
    pos = pos_ref[...]
    wins = _tile_windows(off_sm, bi * ne * (nt + 1) + j, nt + 1, pos, cap, w, rows)
    onehot = _onehots(wins)
    tn = functools.partial(lax.dot_general, dimension_numbers=_TN, preferred_element_type=F32)
    slots = [ref[e, pl.ds(base, w), :] for ref in (yh_ref, yl_ref) for e, (_, base, _) in enumerate(wins)]
    o_ref[...] = tn(jnp.concatenate([onehot, onehot], axis=0), jnp.concatenate(slots, axis=0))

    for e, (p1, base, _) in enumerate(wins):

        @pl.when(p1 > base + w)
        def _():
            def more(lo):
                base2, hit2 = _window_onehot(pos[e:e + 1], lo, cap, w, rows)
                oh = jnp.where(hit2, 1.0, 0.0).astype(BF16)
                o_ref[...] += tn(oh, yh_ref[e, pl.ds(base2, w), :]) + tn(oh, yl_ref[e, pl.ds(base2, w), :])
                return base2 + w

            lax.while_loop(lambda lo: lo < p1, more, base + w)

    o_ref[...] = x1_ref[...] + mod_ref[...][5:6] * o_ref[...]


def _combine(off_flat, x1, mod, stream, pos, ye, t0_blk, t, cap):
    b, s, d = x1.shape
    ne = pos.shape[1]
    tt = min(TOK_TILE, t)
    nt = t // tt
    w = min(SLOT_WINDOW, cap)
    dh = d // COMBINE_SPLIT
    kern = functools.partial(_combine_kernel, cap=cap, nt=nt, w=w)
    grid_spec = pltpu.PrefetchScalarGridSpec(
        num_scalar_prefetch=1,
        grid=(b, COMBINE_SPLIT, nt),
        in_specs=[pl.BlockSpec((None, tt, dh), lambda i, c, j, o: (i, t0_blk * nt + j, c)),
                  pl.BlockSpec((None, None, 6, dh), lambda i, c, j, o: (i, stream, 0, c)),
                  pl.BlockSpec((None, ne, tt), lambda i, c, j, o: (i, 0, j)),
                  pl.BlockSpec((ne, None, cap, dh), lambda i, c, j, o: (0, i, 0, c)),
                  pl.BlockSpec((ne, None, cap, dh), lambda i, c, j, o: (0, i, 0, c))],
        out_specs=pl.BlockSpec((None, tt, dh), lambda i, c, j, o: (i, t0_blk * nt + j, c)))
    return pl.pallas_call(
        kern,
        grid_spec=grid_spec,
        out_shape=jax.ShapeDtypeStruct((b, s, d), F32),
        input_output_aliases={1: 0},
        compiler_params=_cparams("parallel", "parallel", "arbitrary"),
        name="moe_combine",
    )(off_flat, x1, mod, pos, *ye)


def _moe(x1, bm, logits_t, mod, streams, experts):
    b, _, d = x1.shape
    routed = []
    for _, t0_blk, t in streams:
        cap = CAPACITY_FACTOR * t // N_EXPERTS
        pos, aff, offs = _select(logits_t, t0_blk, t, cap)
        nt = t // min(TOK_TILE, t)
        off_flat = offs[:, :, :nt + 1].reshape(-1)
        xe, gs = _gather(off_flat, bm, pos, aff, t0_blk, t, cap)
        routed.append((cap, pos, off_flat, xe.reshape(N_EXPERTS, b * cap, d), gs.reshape(N_EXPERTS, b * cap, 1)))
    yes = _ffn([r[3] for r in routed], [r[4] for r in routed], *experts)
    for (stream, t0_blk, t), (cap, pos, off_flat, _, _), ye in zip(streams, routed, yes):
        ye = [y.reshape(N_EXPERTS, b, cap, d) for y in ye]
        x1 = _combine(off_flat, x1, mod, stream, pos, ye, t0_blk, t, cap)
    return x1


def _rwkv_stream_kernel(*refs, t_lat, has_vres):
    if has_vres:
        (x_ref, xp_ref, xn_ref, mod_ref, g1_ref, mu_ref, wr_ref, wk_ref, wv_ref, w1_ref, w2_ref, w0_ref,
         a1_ref, a2_ref, a0_ref, gg1_ref, gg2_ref, kk_ref, ka_ref, rk_ref, seg_ref, segt_ref,
         vf_ref, v0_ref, v1_ref, v2_ref,
         r_out, v_out, kk_out, lw_out, kd_out, b_out, g_out, bonus_out) = refs
    else:
        (x_ref, xp_ref, xn_ref, mod_ref, g1_ref, mu_ref, wr_ref, wk_ref, wv_ref, w1_ref, w2_ref, w0_ref,
         a1_ref, a2_ref, a0_ref, gg1_ref, gg2_ref, kk_ref, ka_ref, rk_ref, seg_ref, segt_ref,
         r_out, v_out, kk_out, lw_out, kd_out, b_out, g_out, bonus_out) = refs
    j = pl.program_id(1)
    mod = mod_ref[...]
    g1 = g1_ref[...]
    nm = lambda x: _norm_mod(x, g1, mod[1:2], mod[0:1])
    h = nm(x_ref[...])
    tm, d = h.shape
    hp = nm(xp_ref[...])[SUBLANE - 1:SUBLANE]
    hn = nm(xn_ref[...])[0:1]
    row = lax.broadcasted_iota(I32, (tm, 1), 0)
    grow = row + j * tm
    s_tot = pl.num_programs(1) * tm
    first = (grow == 0) | (grow == t_lat)
    last = (grow == t_lat - 1) | (grow == s_tot - 1)
    prev = jnp.where(row == 0, hp, pltpu.roll(h, 1, 0))
    prev = jnp.where(first, 0.0, prev)
    nxt = jnp.where(row == tm - 1, hn, pltpu.roll(h, tm - 1, 0))
    nxt = jnp.where(last, 0.0, nxt)
    xx = 0.5 * (prev + nxt) - h
    mu = mu_ref[...]
    xs = [h + xx * mu[m:m + 1] for m in range(6)]
    r = _bdot(xs[0], wr_ref[...])
    k = _bdot(xs[1], wk_ref[...])
    v = _bdot(xs[2], wv_ref[...])
    if has_vres:
        gate = _sigmoid(v0_ref[...] + _bdot(_bdot(xs[2], v1_ref[...]), v2_ref[...]))
        v = v + (vf_ref[...] - v) * gate
    wz = w0_ref[...] + _bdot(jnp.tanh(_bdot(xs[3], w1_ref[...])), w2_ref[...])
    lw = -math.exp(-0.5) * _sigmoid(wz)
    a = _sigmoid(a0_ref[...] + _bdot(_bdot(xs[4], a1_ref[...]), a2_ref[...]))
    g = _bdot(_sigmoid(_bdot(xs[5], gg1_ref[...])), gg2_ref[...])
    seg, segt = seg_ref[...], segt_ref[...]
    segsum = lambda t: _dot_exact_rhs01(_dot_exact_rhs01(t, seg), segt)
    kkr = k * kk_ref[...]
    kk = kkr / jnp.maximum(jnp.sqrt(segsum(kkr * kkr)), 1e-12)
    ka = ka_ref[...]
    kd0 = k * (1.0 + (a[:, :d] - 1.0) * ka)
    kd1 = k * (1.0 + (a[:, d:] - 1.0) * ka)
    r_out[...] = r
    v_out[...] = v
    kk_out[...] = kk
    lw_out[...] = lw
    kd_out[:, :d] = kd0
    kd_out[:, d:] = kd1
    b_out[:, :d] = kk * a[:, :d]
    b_out[:, d:] = kk * a[:, d:]
    g_out[...] = g
    bonus_out[...] = segsum(r * (kd0 + kd1) * rk_ref[...]) * v


def _rwkv_stream(x, mod, g1, wts, vres, t_lat):
    b, s, d = x.shape
    tm = TOK_TILE
    nlt = t_lat // tm
    nsub = tm // SUBLANE
    const = lambda a: pl.BlockSpec(a.shape, lambda i, j: (0,) * a.ndim)
    tok = lambda w: pl.BlockSpec((None, tm, w), lambda i, j: (i, j, 0))
    in_specs = [tok(d),
                pl.BlockSpec((None, SUBLANE, d), lambda i, j: (i, jnp.maximum(j * nsub - 1, 0), 0)),
                pl.BlockSpec((None, SUBLANE, d), lambda i, j: (i, jnp.minimum((j + 1) * nsub, s // SUBLANE - 1), 0)),
                pl.BlockSpec((None, None, 6, d), lambda i, j: (i, (j >= nlt).astype(I32), 0, 0)),
                const(g1)] + [const(a) for a in wts]
    args = [x, x, x, mod, g1] + list(wts)
    if vres is not None:
        vf, v0, v1, v2 = vres
        in_specs += [tok(d), const(v0), const(v1), const(v2)]
        args += [vf, v0, v1, v2]
    kern = functools.partial(_rwkv_stream_kernel, t_lat=t_lat, has_vres=vres is not None)
    widths = [d, d, d, 2 * d, 2 * d, 2 * d, d, d]
    return pl.pallas_call(
        kern,
        grid=(b, s // tm),
        in_specs=in_specs,
        out_specs=[tok(w) for w in widths],
        out_shape=[jax.ShapeDtypeStruct((b, s, w), F32) for w in widths],
        compiler_params=_cparams("parallel", "parallel"),
        name="rwkv_stream",
    )(*args)


def _scan_kernel(r_ref, v_ref, kk_ref, lw_ref, kd_ref, b_ref, y_ref, h_ref, *, reverse, nchunk):
    L = SCAN_CHUNK

    @pl.when(pl.program_id(2) == 0)
    def _():
        h_ref[...] = jnp.zeros_like(h_ref)

    n2 = 2 * L
    rr = lax.broadcasted_iota(I32, (n2, n2), 0)
    cc = lax.broadcasted_iota(I32, (n2, n2), 1)
    tr, tc = rr % L, cc % L
    same = (rr // L) == (cc // L)
    before = (tr < tc) if reverse else (tr > tc)
    strict = same & before
    incl = same & (before | (tr == tc))
    eye = rr == cc
    blk16 = (rr // 16) == (cc // 16)
    blk32 = (rr // 32) == (cc // 32)
    ri = lax.broadcasted_iota(I32, (L, L), 0)
    ci = lax.broadcasted_iota(I32, (L, L), 1)
    tri = jnp.where((ci >= ri) if reverse else (ci <= ri), 1.0, 0.0).astype(BF16)
    head0 = lax.broadcasted_iota(I32, (L, LANE), 1) < RW_HEAD

    def stack(x):
        return jnp.concatenate([jnp.where(head0, x, 0.0), jnp.where(head0, 0.0, x)], axis=0)

    order = list(range(nchunk - 1, -1, -1) if reverse else range(nchunk))
    npp = y_ref.shape[-1] // LANE
    units = [(pp, ch) for pp in range(npp) for ch in order]

    def each(f, *lists):
        return [f(*a) for a in zip(*lists)]

    def ld(ref):
        return [ref[pl.ds(ch * L, L), pl.ds(pp * LANE, LANE)] for pp, ch in units]

    dd = functools.partial(jnp.dot, preferred_element_type=F32)
    lw = ld(lw_ref)
    lws = each(_split3, lw)
    cs = each(lambda t: dd(tri, t[0]) + (dd(tri, t[1]) + dd(tri, t[2])), lws)
    total = each(lambda c: c[0:1] if reverse else c[L - 1:L], cs)
    gam = each(jnp.exp, cs)
    ginv = each(lambda c: jnp.exp(-c), cs)
    gprev = each(lambda c, l: jnp.exp(c - l), cs, lw)
    gend = each(lambda t, c: jnp.exp(t - c), total, cs)
    kk, bb, kd, r, v = ld(kk_ref), ld(b_ref), ld(kd_ref), ld(r_ref), ld(v_ref)
    mul_stack = lambda a, g: stack(a * g)
    rtm = each(mul_stack, r, gam)
    rts = each(_split2, rtm)
    kkm, btm, ktm = (each(_split2, each(mul_stack, a, g)) for a, g in ((kk, gprev), (bb, ginv), (kd, ginv)))
    bht, kht = (each(lambda a, g: _split2(stack(a * g).T), a, gend) for a in (bb, kd))
    vm = each(_split2, each(stack, v))
    rows2 = lambda a, c: tuple(jnp.concatenate([s, t], axis=0) for s, t in zip(a, c))
    nm_abak = each(lambda a, c, e, f: _mm3_rows([a, c], rows2(e, f), _NT), kkm, rts, btm, ktm)
    n = each(lambda t: jnp.where(strict, t[0][:, :n2], 0.0), nm_abak)
    m = each(lambda t: jnp.where(strict, t[0][:, n2:], 0.0), nm_abak)
    ab = each(lambda t: jnp.where(incl, t[1][:, :n2], 0.0), nm_abak)
    ak = each(lambda t: jnp.where(incl, t[1][:, n2:], 0.0), nm_abak)
    nd = each(lambda t: jnp.where(blk16, t, 0.0), n)
    x = each(lambda t: jnp.where(eye, 1.0, 0.0) - t, nd)
    pw = nd
    for _ in range(3):
        pw = each(lambda t: _bdot(t, t), pw)
        x = each(lambda a, c: a + _bdot(a, c), x, pw)
    ident = jnp.where(eye, 1.0, 0.0)
    for inner, outer in ((blk16, blk32), (blk32, same)):
        c = each(lambda t: jnp.where(outer & jnp.logical_not(inner), t, 0.0), n)
        e = each(lambda a: a - ident, x)
        ec = each(_bdot, e, c)
        x = each(lambda a, t, u, s: a - t - (u + _bdot(t + u, s)), x, c, ec, e)
    xs = each(_split2, x)
    abs_ = each(_split2, ab)
    mv_akv_khv = each(lambda a, c, e, f: _mm3_rows([_split2(a), _split2(c), e], f), m, ak, kht, vm)
    cat = lambda a, c: tuple(jnp.concatenate([s, t], axis=1) for s, t in zip(a, c))
    wu = each(lambda a, c, t: _split2(_mm3(a, cat(c, _split2(t[0])))), xs, kkm, mv_akv_khv)
    bw_abw = each(lambda a, c, e: _mm3_rows([a, c], e), bht, abs_, wu)
    p = each(lambda t, a: jnp.where(eye, jnp.exp(t), 0.0) - a[0][:, :LANE], total, bw_abw)
    q = each(lambda t, a: t[2] - a[0][:, LANE:], mv_akv_khv, bw_abw)
    rres = each(lambda t, a: t - a[1][:, :LANE], rtm, bw_abw)
    y0 = each(lambda t, a: t[1] - a[1][:, LANE:], mv_akv_khv, bw_abw)
    for i, (pp, ch) in enumerate(units):
        y, hn = _mm3_rows([_split2(rres[i]), _split2(p[i])], _split2(h_ref[pp]))
        y = y + y0[i]
        y_ref[pl.ds(ch * L, L), pl.ds(pp * LANE, LANE)] = y[:L] + y[L:]
        h_ref[pp] = hn + q[i]


def _scan(r, v, kk, lw, kd, bb, t_lat, reverse):
    b, s, d = r.shape
    tb = TOK_TILE
    nl, nc = t_lat // tb, (s - t_lat) // tb
    z = 1 if reverse else 0

    def blk(c):
        if reverse:
            return jnp.where(c < nc, nl + nc - 1 - c, nl - 1 - (c - nc))
        return jnp.where(c < nc, nl + c, c - nc)

    wl = SCAN_PAIRS * LANE
    ngrp = d // wl
    shared = pl.BlockSpec((None, tb, wl), lambda i, p, c: (i, blk(c), p))
    dirn = pl.BlockSpec((None, tb, wl), lambda i, p, c: (i, blk(c), z * ngrp + p))
    kern = functools.partial(_scan_kernel, reverse=reverse, nchunk=tb // SCAN_CHUNK)
    return pl.pallas_call(
        kern,
        grid=(b, ngrp, nl + nc),
        in_specs=[shared, shared, shared, dirn, dirn, dirn],
        out_specs=pl.BlockSpec((None, tb, wl), lambda i, p, c: (i, blk(c), p)),
        out_shape=jax.ShapeDtypeStruct((b, s, d), F32),
        scratch_shapes=[pltpu.VMEM((SCAN_PAIRS, LANE, LANE), F32)],
        compiler_params=_cparams("parallel", "parallel", "arbitrary"),
        name="wkv_scan_bwd" if reverse else "wkv_scan_fwd",
    )(r, v, kk, lw, kd, bb)


def _final_kernel(x_ref, g_ref, o_ref):
    o_ref[...] = _rms(x_ref[...], g_ref[...])


def _final_norm(x, g, t_lat):
    b, _, d = x.shape
    tm = TOK_TILE
    return pl.pallas_call(
        _final_kernel,
        grid=(b, t_lat // tm),
        in_specs=[pl.BlockSpec((None, tm, d), lambda i, j: (i, j, 0)),
                  pl.BlockSpec(g.shape, lambda i, j: (0, 0))],
        out_specs=pl.BlockSpec((None, tm, d), lambda i, j: (i, j, 0)),
        out_shape=jax.ShapeDtypeStruct((b, t_lat, d), F32),
        compiler_params=_cparams("parallel", "parallel"),
        name="final_norm",
    )(x, g)


def _rope_tables(t_lat, t_ctx):
    rows = t_lat // GRID_W
    r = jnp.broadcast_to(jnp.arange(rows)[:, None], (rows, GRID_W)).reshape(-1).astype(F32)
    col = jnp.broadcast_to(jnp.arange(GRID_W)[None, :], (rows, GRID_W)).reshape(-1).astype(F32)
    n_freq = MLA_ROPE // 4
    inv = ROPE_THETA ** (-jnp.arange(n_freq, dtype=F32) / n_freq)
    ang = jnp.concatenate([r[:, None] * inv, col[:, None] * inv], axis=-1)
    cos, sin = jnp.cos(ang), jnp.sin(ang)
    cosf = jnp.concatenate([cos, cos, cos, cos], axis=-1)
    sins = jnp.concatenate([-sin, sin, -sin, sin], axis=-1)
    cosf = jnp.concatenate([cosf, jnp.ones((t_ctx, LANE), F32)], axis=0)
    sins = jnp.concatenate([sins, jnp.zeros((t_ctx, LANE), F32)], axis=0)
    return cosf, sins


def _blockdiag2(w):
    z = jnp.zeros_like(w[0])
    return jnp.concatenate([jnp.concatenate([w[0], z], axis=1), jnp.concatenate([z, w[1]], axis=1)], axis=0)


def kernel(x, c, ctx, c_ctx, mod_w, mod_b, norm1_g, norm2_g, final_g, mla_w_in, mla_g_q, mla_g_kv, mla_w_uq, mla_w_ukv, mla_w_o, rw_mu, rw_w_rkv, rw_w0, rw_w1, rw_w2, rw_a0, rw_a1, rw_a2, rw_v0, rw_v1, rw_v2, rw_g1, rw_g2, rw_k_k, rw_k_a, rw_r_k, rw_lnx_g, rw_lnx_b, rw_w_o, moe_router, moe_w1, moe_w3, moe_w2):
    b, t_lat, d = x.shape
    t_ctx = ctx.shape[1]
    depth = mod_w.shape[0]
    assert b + 1 <= SUBLANE and t_lat % TOK_TILE == 0 and t_ctx % TOK_TILE == 0 and d % LANE == 0
    n_lat_tiles = t_lat // TOK_TILE
    ctx_blk = t_lat // t_ctx
    assert ctx_blk * t_ctx == t_lat

    cc = jnp.concatenate([c, c_ctx[None], jnp.zeros((SUBLANE - b - 1, d), F32)], axis=0)
    mods = _modulation(cc, mod_w, mod_b).reshape(depth, SUBLANE, 6, d)
    mod_all = jnp.stack([mods[:, :b], jnp.broadcast_to(mods[:, b:b + 1], (depth, b, 6, d))], axis=2)

    xs = jnp.concatenate([x, ctx], axis=1)
    cosf, sins = _rope_tables(t_lat, t_ctx)
    row = lambda a: a.reshape(1, -1)
    seg = (jnp.arange(d)[:, None] // RW_HEAD == jnp.arange(LANE)[None, :]).astype(BF16)
    segt = seg.T

    v_first = None
    for layer in range(depth):
        need_ctx = layer < depth - 1
        mod = mod_all[layer]
        j = layer // 2
        if layer % 2 == 0:
            win = jnp.pad(mla_w_in[j], ((0, 0), (0, 4 * LANE - mla_w_in.shape[-1]))).astype(BF16)
            wuq = mla_w_uq[j].reshape(MLA_Q_LORA, MLA_HEADS, MLA_NOPE + MLA_ROPE)
            wuq = jnp.concatenate([wuq[..., :MLA_NOPE].reshape(MLA_Q_LORA, -1),
                                   wuq[..., MLA_NOPE:].reshape(MLA_Q_LORA, -1)], axis=1).astype(BF16)
            q, k, v = _mla_proj(xs, mod, row(norm1_g[layer]), win, row(mla_g_q[j]), row(mla_g_kv[j]),
                                wuq, mla_w_ukv[j].astype(BF16), cosf, sins, n_lat_tiles)
            mix = [_attention(q, k, v, t_lat)]
            wo = mla_w_o[j].astype(BF16)
        else:
            lora = lambda w: jnp.concatenate([w[0], w[1]], axis=1).astype(BF16)
            wts = [rw_mu[j], rw_w_rkv[j, 0].astype(BF16), rw_w_rkv[j, 1].astype(BF16), rw_w_rkv[j, 2].astype(BF16),
                   lora(rw_w1[j]), _blockdiag2(rw_w2[j]).astype(BF16), rw_w0[j].reshape(1, -1),
                   lora(rw_a1[j]), _blockdiag2(rw_a2[j]).astype(BF16), rw_a0[j].reshape(1, -1),
                   rw_g1[j].astype(BF16), rw_g2[j].astype(BF16),
                   row(rw_k_k[j]), row(rw_k_a[j]), row(rw_r_k[j]), seg, segt]
            vres = None
            if j > 0:
                pad = LANE - rw_v1.shape[-1]
                vres = (v_first, row(rw_v0[j - 1]), jnp.pad(rw_v1[j - 1], ((0, 0), (0, pad))).astype(BF16),
                        jnp.pad(rw_v2[j - 1], ((0, pad), (0, 0))).astype(BF16))
            r, v, kk, lw, kd, bb, g, bonus = _rwkv_stream(xs, mod, row(norm1_g[layer]), wts, vres, t_lat)
            if j == 0:
                v_first = v
            yf = _scan(r, v, kk, lw, kd, bb, t_lat, reverse=False)
            yb = _scan(r, v, kk, lw, kd, bb, t_lat, reverse=True)
            mix = [yf, yb, bonus, g, row(rw_lnx_g[j]), row(rw_lnx_b[j]), seg, segt]
            wo = rw_w_o[j].astype(BF16)
        x1, bm, logits_t = _post_mixer(xs, mix, wo, mod, row(norm2_g[layer]), moe_router[layer].T.astype(BF16),
                                       n_lat_tiles)
        experts = (moe_w1, moe_w3, moe_w2, layer)
        streams = [(0, 0, t_lat)] + ([(1, ctx_blk, t_ctx)] if need_ctx else [])
        xs = _moe(x1, bm, logits_t, mod, streams, experts)
    return _final_norm(xs, row(final_g), t_lat)
```
